```python
import jax
import jax.numpy as jnp
from jax import lax
import numpy as np

D_MODEL = 4096
BATCH = 4
SEQ = 2048
DEPTH = 1

CHUNK = 64
EPS = 1e-6
FFN_RES = 0.5
D_FF = 11008
D_CONV = 2048
CONV_W = 3
GLA_HEADS = 4
D_GLA_K = 1024
D_GLA_V = 2048
HEAD_K = D_GLA_K // GLA_HEADS
HEAD_V = D_GLA_V // GLA_HEADS
GATE_RANK = 16
GATE_TEMP = 16.0
N_BRANCH = 2
IN_SPLIT_SIZES = (D_CONV, D_CONV, D_CONV, D_GLA_K, D_GLA_K, D_GLA_V, D_GLA_V, GATE_RANK, D_MODEL, D_MODEL)
D_IN = 3 * D_CONV + 2 * D_GLA_K + 2 * D_GLA_V + GATE_RANK + N_BRANCH * D_MODEL

kernel_name = "hybrid_shortconv_gla_macaron_block"


def _rmsnorm(x, g):
    xf = x.astype(jnp.float32)
    y = xf * lax.rsqrt(jnp.mean(xf * xf, axis=-1, keepdims=True) + EPS)
    return (y * g.astype(jnp.float32)).astype(x.dtype)


def _swiglu(h, w_gate, w_up, w_down):
    return (jax.nn.silu(h @ w_gate) * (h @ w_up)) @ w_down


def _causal_short_conv(u, w, b):
    s = u.shape[1]
    up = jnp.pad(u, ((0, 0), (CONV_W - 1, 0), (0, 0)))
    out = b
    for tap in range(CONV_W):
        out = out + w[tap] * up[:, tap:tap + s, :]
    return out


def _gla_chunk_step(state, chunk):
    q, k, v, lcum = chunk
    decay = jnp.exp(-jnp.abs(lcum[:, :, :, None, :] - lcum[:, :, None, :, :]))
    scores = jnp.einsum('bhid,bhjd,bhijd->bhij', q, k, decay)
    o = jnp.einsum('bhij,bhjv->bhiv', scores, v) + jnp.einsum('bhid,bhdv->bhiv', q * jnp.exp(lcum), state)
    l_last = lcum[:, :, -1, :]
    k_dec = k * jnp.exp(l_last[:, :, None, :] - lcum)
    new_state = jnp.exp(l_last)[..., None] * state + jnp.einsum('bhjd,bhjv->bhdv', k_dec, v)
    return new_state, o


def _gla(q, k, v, log_alpha):
    b, s, _ = q.shape
    n = s // CHUNK

    def to_chunks(t, hd):
        return t.reshape(b, n, CHUNK, GLA_HEADS, hd).transpose(1, 0, 3, 2, 4).astype(jnp.float32)

    qc = to_chunks(q, HEAD_K) * (HEAD_K ** -0.5)
    kc = to_chunks(k, HEAD_K)
    vc = to_chunks(v, HEAD_V)
    lc = jnp.cumsum(to_chunks(log_alpha, HEAD_K), axis=3)
    s0 = jnp.zeros((b, GLA_HEADS, HEAD_K, HEAD_V), jnp.float32)
    _, o = lax.scan(_gla_chunk_step, s0, (qc, kc, vc, lc))
    return o.transpose(1, 0, 3, 2, 4).reshape(b, s, GLA_HEADS, HEAD_V)


def _mixer(h, w_in, conv_w, conv_b, w_conv_out, w_alpha_up, b_alpha, gla_norm_g, w_gla_out, b_merge, w_mix_out):
    b, s, _ = h.shape
    split_points = tuple(np.cumsum(IN_SPLIT_SIZES)[:-1].tolist())
    proj = h @ w_in
    cb, cc, cu, q, k, v, r, a_low, g_a, g_b = jnp.split(proj, split_points, axis=-1)
    y_a = (cb * _causal_short_conv(cc * cu, conv_w, conv_b)) @ w_conv_out
    log_alpha = jax.nn.log_sigmoid((a_low @ w_alpha_up + b_alpha).astype(jnp.float32)) / GATE_TEMP
    o = _rmsnorm(_gla(q, k, v, log_alpha), gla_norm_g)
    o = o.astype(h.dtype).reshape(b, s, D_GLA_V) * jax.nn.silu(r)
    y_b = o @ w_gla_out
    merged = jax.nn.sigmoid(g_a + b_merge[0]) * y_a + jax.nn.sigmoid(g_b + b_merge[1]) * y_b
    return merged @ w_mix_out


def setup_inputs(seed: int = 0) -> dict:
    key = jax.random.key(seed)
    ks = jax.random.split(key, 22)

    def nrm(k, shape, scale):
        return jax.random.normal(k, shape, jnp.float32) * scale

    def gain(k, shape):
        return 1.0 + 0.02 * jax.random.normal(k, shape, jnp.float32)

    L = DEPTH
    return {
        "x": nrm(ks[0], (BATCH, SEQ, D_MODEL), 1.0),
        "ffn1_norm_g": gain(ks[1], (L, D_MODEL)),
        "ffn1_w_gate": nrm(ks[2], (L, D_MODEL, D_FF), D_MODEL ** -0.5),
        "ffn1_w_up": nrm(ks[3], (L, D_MODEL, D_FF), D_MODEL ** -0.5),
        "ffn1_w_down": nrm(ks[4], (L, D_FF, D_MODEL), D_FF ** -0.5),
        "mix_norm_g": gain(ks[5], (L, D_MODEL)),
        "w_in": nrm(ks[6], (L, D_MODEL, D_IN), D_MODEL ** -0.5),
        "conv_w": nrm(ks[7], (L, CONV_W, D_CONV), CONV_W ** -0.5),
        "conv_b": nrm(ks[8], (L, D_CONV), 0.02),
        "w_conv_out": nrm(ks[9], (L, D_CONV, D_MODEL), D_CONV ** -0.5),
        "w_alpha_up": nrm(ks[10], (L, GATE_RANK, D_GLA_K), GATE_RANK ** -0.5),
        "b_alpha": nrm(ks[11], (L, D_GLA_K), 0.02),
        "gla_norm_g": gain(ks[12], (L, HEAD_V)),
        "w_gla_out": nrm(ks[13], (L, D_GLA_V, D_MODEL), D_GLA_V ** -0.5),
        "b_merge": nrm(ks[14], (L, N_BRANCH, D_MODEL), 0.02),
        "w_mix_out": nrm(ks[15], (L, D_MODEL, D_MODEL), D_MODEL ** -0.5),
        "ffn2_norm_g": gain(ks[16], (L, D_MODEL)),
        "ffn2_w_gate": nrm(ks[17], (L, D_MODEL, D_FF), D_MODEL ** -0.5),
        "ffn2_w_up": nrm(ks[18], (L, D_MODEL, D_FF), D_MODEL ** -0.5),
        "ffn2_w_down": nrm(ks[19], (L, D_FF, D_MODEL), D_FF ** -0.5),
        "final_norm_g": gain(ks[20], (D_MODEL,)),
    }


def reference(x, ffn1_norm_g, ffn1_w_gate, ffn1_w_up, ffn1_w_down, mix_norm_g, w_in, conv_w, conv_b,
              w_conv_out, w_alpha_up, b_alpha, gla_norm_g, w_gla_out, b_merge, w_mix_out,
              ffn2_norm_g, ffn2_w_gate, ffn2_w_up, ffn2_w_down, final_norm_g):
    for l in range(DEPTH):
        x = x + FFN_RES * _swiglu(_rmsnorm(x, ffn1_norm_g[l]), ffn1_w_gate[l], ffn1_w_up[l], ffn1_w_down[l])
        x = x + _mixer(_rmsnorm(x, mix_norm_g[l]), w_in[l], conv_w[l], conv_b[l], w_conv_out[l],
                       w_alpha_up[l], b_alpha[l], gla_norm_g[l], w_gla_out[l], b_merge[l], w_mix_out[l])
        x = x + FFN_RES * _swiglu(_rmsnorm(x, ffn2_norm_g[l]), ffn2_w_gate[l], ffn2_w_up[l], ffn2_w_down[l])
    return _rmsnorm(x, final_norm_g)
```

```python
import functools

import jax
import jax.numpy as jnp
import numpy as np
from jax import lax
from jax.experimental import pallas as pl
from jax.experimental.pallas import tpu as pltpu

F32 = jnp.float32
BF16 = jnp.bfloat16

EPS = 1e-6
FFN_RES = 0.5
CHUNK = 64
CONV_W = 3
GLA_HEADS = 4
GATE_TEMP = 16.0
N_LEVELS = 6
FFN_DOWN_COLS = 1024

V7X_VMEM_LIMIT_BYTES = 58 * 1024 * 1024
LANES = 128


def _params(*sem):
    return pltpu.CompilerParams(dimension_semantics=sem, vmem_limit_bytes=V7X_VMEM_LIMIT_BYTES)


def _rms(x, g):
    ms = jnp.mean(x * x, axis=-1, keepdims=True)
    return x * lax.rsqrt(ms + EPS) * g


def _ffn_body(x_ref, g_ref, wg_ref, wu_ref, wd_ref, o_ref, h_ref):
    f = pl.program_id(1)

    @pl.when(f == 0)
    def _():
        h_ref[...] = _rms(x_ref[...], g_ref[...]).astype(BF16)
        o_ref[...] = jnp.zeros_like(o_ref)

    h = h_ref[...]
    gate = jnp.dot(h, wg_ref[...], preferred_element_type=F32)
    up = jnp.dot(h, wu_ref[...], preferred_element_type=F32)
    act = (gate * jax.nn.sigmoid(gate) * up).astype(BF16)
    d = o_ref.shape[1]
    for n0 in range(0, d, FFN_DOWN_COLS):
        sl = slice(n0, n0 + FFN_DOWN_COLS)
        o_ref[:, sl] += jnp.dot(act, wd_ref[:, sl], preferred_element_type=F32)

    @pl.when(f == pl.num_programs(1) - 1)
    def _():
        o_ref[...] = x_ref[...] + FFN_RES * o_ref[...]


def _ffn(x, g, wg, wu, wd, *, bm=512, bf=256):
    m, d = x.shape
    dff = wg.shape[1]
    return pl.pallas_call(
        _ffn_body,
        grid=(m // bm, dff // bf),
        in_specs=[
            pl.BlockSpec((bm, d), lambda i, f: (i, 0)),
            pl.BlockSpec((1, d), lambda i, f: (0, 0)),
            pl.BlockSpec((d, bf), lambda i, f: (0, f)),
            pl.BlockSpec((d, bf), lambda i, f: (0, f)),
            pl.BlockSpec((bf, d), lambda i, f: (f, 0)),
        ],
        out_specs=pl.BlockSpec((bm, d), lambda i, f: (i, 0)),
        out_shape=jax.ShapeDtypeStruct((m, d), F32),
        scratch_shapes=[pltpu.VMEM((bm, d), BF16)],
        compiler_params=_params("parallel", "arbitrary"),
        name="ffn",
    )(x, g, wg, wu, wd)


def _norm_body(x_ref, g_ref, o_ref):
    o_ref[...] = _rms(x_ref[...], g_ref[...]).astype(o_ref.dtype)


def _norm(x, g, out_dtype, *, bm=512):
    m, d = x.shape
    return pl.pallas_call(
        _norm_body,
        grid=(m // bm,),
        in_specs=[pl.BlockSpec((bm, d), lambda i: (i, 0)), pl.BlockSpec((1, d), lambda i: (0, 0))],
        out_specs=pl.BlockSpec((bm, d), lambda i: (i, 0)),
        out_shape=jax.ShapeDtypeStruct((m, d), out_dtype),
        compiler_params=_params("parallel"),
        name="norm",
    )(x, g)


def _mm_body(a_ref, w_ref, o_ref):
    o_ref[...] = jnp.dot(a_ref[...], w_ref[...], preferred_element_type=F32).astype(o_ref.dtype)


def _mm_res_body(a_ref, w_ref, r_ref, o_ref):
    acc = jnp.dot(a_ref[...], w_ref[...], preferred_element_type=F32)
    o_ref[...] = (r_ref[...] + acc).astype(o_ref.dtype)


def _matmul(a, w, out_dtype, *, res=None, bm=1024, bn=512, name="matmul"):
    m, k = a.shape
    n = w.shape[1]
    bn = min(bn, n)
    in_specs = [pl.BlockSpec((bm, k), lambda i, j: (i, 0)), pl.BlockSpec((k, bn), lambda i, j: (0, j))]
    args = [a, w]
    body = _mm_body
    if res is not None:
        in_specs.append(pl.BlockSpec((bm, bn), lambda i, j: (i, j)))
        args.append(res)
        body = _mm_res_body
    return pl.pallas_call(
        body,
        grid=(m // bm, n // bn),
        in_specs=in_specs,
        out_specs=pl.BlockSpec((bm, bn), lambda i, j: (i, j)),
        out_shape=jax.ShapeDtypeStruct((m, n), out_dtype),
        compiler_params=_params("parallel", "arbitrary"),
        name=name,
    )(*args)


def _conv_body(cb_ref, cc_ref, cu_ref, w_ref, b_ref, o_ref):
    p = cc_ref[0].astype(F32) * cu_ref[0].astype(F32)
    row = lax.broadcasted_iota(jnp.int32, p.shape, 0)
    p1 = jnp.where(row >= 1, pltpu.roll(p, 1, axis=0), 0.0)
    p2 = jnp.where(row >= 2, pltpu.roll(p, 2, axis=0), 0.0)
    w = w_ref[...]
    conv = b_ref[...] + w[0:1, :] * p2
    conv = conv + w[1:2, :] * p1
    conv = conv + w[2:3, :] * p
    o_ref[0] = (cb_ref[0].astype(F32) * conv).astype(o_ref.dtype)


def _conv_branch(proj3, conv_w, conv_b, d_conv, *, bc=256):
    b, s, _ = proj3.shape
    nb = d_conv // bc
    return pl.pallas_call(
        _conv_body,
        grid=(b, nb),
        in_specs=[
            pl.BlockSpec((1, s, bc), lambda i, j: (i, 0, j)),
            pl.BlockSpec((1, s, bc), lambda i, j: (i, 0, nb + j)),
            pl.BlockSpec((1, s, bc), lambda i, j: (i, 0, 2 * nb + j)),
            pl.BlockSpec((CONV_W, bc), lambda i, j: (0, j)),
            pl.BlockSpec((1, bc), lambda i, j: (0, j)),
        ],
        out_specs=pl.BlockSpec((1, s, bc), lambda i, j: (i, 0, j)),
        out_shape=jax.ShapeDtypeStruct((b, s, d_conv), BF16),
        compiler_params=_params("parallel", "parallel"),
        name="conv",
    )(proj3, proj3, proj3, conv_w, conv_b)


def _gla_tables():
    c = CHUNK
    i = np.arange(c)[:, None]
    t = np.arange(c)[None, :]
    blocks = [(t <= i), (t > i)]
    masks = []
    for lvl in range(N_LEVELS):
        p = c >> lvl
        half = p // 2
        mid = (i // p) * p + half
        upper = i >= mid
        blocks.append(np.where(upper, (t > mid) & (t <= i), (t > i) & (t <= mid)))
        j = t
        same_parent = (i // p) == (j // p)
        masks.append(same_parent & (((i % p) >= half) != ((j % p) >= half)))
    expo = np.concatenate(blocks, axis=0).astype(np.float32)
    mask = np.stack(masks).astype(np.float32)
    return expo, mask


def _gla_body(q_ref, k_ref, v_ref, r_ref, al_ref, wup_ref, ba_ref, gn_ref, expo_ref, mask_ref,
              o_ref, state_ref, *, scale):
    c = CHUNK

    @pl.when(pl.program_id(2) == 0)
    def _():
        state_ref[...] = jnp.zeros_like(state_ref)

    q = q_ref[0].astype(F32) * scale
    k = k_ref[0].astype(F32)
    v = v_ref[0]

    z = jnp.dot(al_ref[0], wup_ref[...], preferred_element_type=F32,
                precision=lax.Precision.HIGHEST) + ba_ref[...]
    la = (jnp.minimum(z, 0.0) - jnp.log1p(jnp.exp(-jnp.abs(z)))) / GATE_TEMP
    e_all = jnp.exp(jnp.dot(expo_ref[...], la, preferred_element_type=F32,
                            precision=lax.Precision.HIGHEST))
    e_l = e_all[0:c]
    e_rest = e_all[c:2 * c]

    nt = (((1,), (1,)), ((), ()))
    row = lax.broadcasted_iota(jnp.int32, (c, c), 0)
    col = lax.broadcasted_iota(jnp.int32, (c, c), 1)
    scores = jnp.where(row == col, jnp.sum(q * k, axis=-1, keepdims=True), 0.0)
    for lvl in range(N_LEVELS):
        e = e_all[(2 + lvl) * c:(3 + lvl) * c]
        p = lax.dot_general((q * e).astype(BF16), (k * e).astype(BF16), nt, preferred_element_type=F32)
        scores = scores + mask_ref[lvl] * p

    state = state_ref[...]
    o = jnp.dot(scores.astype(BF16), v, preferred_element_type=F32)
    o = o + lax.dot_general((q * e_l).astype(BF16), state.astype(BF16), nt, preferred_element_type=F32)

    k_dec = (k * e_rest).astype(BF16)
    tn = (((0,), (0,)), ((), ()))
    state_ref[...] = state * e_l[c - 1:c, :] + lax.dot_general(v, k_dec, tn, preferred_element_type=F32)

    r = r_ref[0].astype(F32)
    o_ref[0] = (_rms(o, gn_ref[...]) * (r * jax.nn.sigmoid(r))).astype(o_ref.dtype)


def _gla(proj3, a_low3, w_up, b_alpha, gn, *, q_off, k_off, v_off, r_off, hk, hv):
    b, s, _ = proj3.shape
    nc = s // CHUNK
    expo, mask = _gla_tables()
    rank_pad = a_low3.shape[-1]
    body = functools.partial(_gla_body, scale=float(hk) ** -0.5)
    return pl.pallas_call(
        body,
        grid=(b, GLA_HEADS, nc),
        in_specs=[
            pl.BlockSpec((1, CHUNK, hk), lambda i, h, c: (i, c, q_off // hk + h)),
            pl.BlockSpec((1, CHUNK, hk), lambda i, h, c: (i, c, k_off // hk + h)),
            pl.BlockSpec((1, CHUNK, hv), lambda i, h, c: (i, c, v_off // hv + h)),
            pl.BlockSpec((1, CHUNK, hv), lambda i, h, c: (i, c, r_off // hv + h)),
            pl.BlockSpec((1, CHUNK, rank_pad), lambda i, h, c: (i, c, 0)),
            pl.BlockSpec((rank_pad, hk), lambda i, h, c: (0, h)),
            pl.BlockSpec((1, hk), lambda i, h, c: (0, h)),
            pl.BlockSpec((1, hv), lambda i, h, c: (0, 0)),
            pl.BlockSpec(expo.shape, lambda i, h, c: (0, 0)),
            pl.BlockSpec(mask.shape, lambda i, h, c: (0, 0, 0)),
        ],
        out_specs=pl.BlockSpec((1, CHUNK, hv), lambda i, h, c: (i, c, h)),
        out_shape=jax.ShapeDtypeStruct((b, s, GLA_HEADS * hv), BF16),
        scratch_shapes=[pltpu.VMEM((hv, hk), F32)],
        compiler_params=_params("parallel", "parallel", "arbitrary"),
        name="gla",
    )(proj3, proj3, proj3, proj3, a_low3, w_up, b_alpha, gn, jnp.asarray(expo), jnp.asarray(mask))


def _merge_body(ua_ref, ub_ref, wa_ref, wb_ref, ga_ref, gb_ref, bm_ref, o_ref):
    ya = jnp.dot(ua_ref[...], wa_ref[...], preferred_element_type=F32)
    yb = jnp.dot(ub_ref[...], wb_ref[...], preferred_element_type=F32)
    bias = bm_ref[...]
    sa = jax.nn.sigmoid(ga_ref[...].astype(F32) + bias[0:1, :])
    sb = jax.nn.sigmoid(gb_ref[...].astype(F32) + bias[1:2, :])
    o_ref[...] = (sa * ya + sb * yb).astype(o_ref.dtype)


def _merge(ua, ub, wa, wb, gates, b_merge, *, bm=1024, bn=512):
    m, kk = ua.shape
    n = wa.shape[1]
    nb = n // bn
    return pl.pallas_call(
        _merge_body,
        grid=(m // bm, nb),
        in_specs=[
            pl.BlockSpec((bm, kk), lambda i, j: (i, 0)),
            pl.BlockSpec((bm, kk), lambda i, j: (i, 0)),
            pl.BlockSpec((kk, bn), lambda i, j: (0, j)),
            pl.BlockSpec((kk, bn), lambda i, j: (0, j)),
            pl.BlockSpec((bm, bn), lambda i, j: (i, j)),
            pl.BlockSpec((bm, bn), lambda i, j: (i, nb + j)),
            pl.BlockSpec((2, bn), lambda i, j: (0, j)),
        ],
        out_specs=pl.BlockSpec((bm, bn), lambda i, j: (i, j)),
        out_shape=jax.ShapeDtypeStruct((m, n), BF16),
        compiler_params=_params("parallel", "arbitrary"),
        name="merge",
    )(ua, ub, wa, wb, gates, gates, b_merge)


def kernel(x, ffn1_norm_g, ffn1_w_gate, ffn1_w_up, ffn1_w_down, mix_norm_g, w_in, conv_w, conv_b, w_conv_out,
           w_alpha_up, b_alpha, gla_norm_g, w_gla_out, b_merge, w_mix_out, ffn2_norm_g, ffn2_w_gate, ffn2_w_up,
           ffn2_w_down, final_norm_g):
    batch, seq, d_model = x.shape
    depth = ffn1_w_gate.shape[0]
    d_conv = conv_w.shape[-1]
    rank, d_gla_k = w_alpha_up.shape[-2:]
    d_gla_v = w_gla_out.shape[-2]
    hk, hv = d_gla_k // GLA_HEADS, d_gla_v // GLA_HEADS
    q_off = 3 * d_conv
    k_off = q_off + d_gla_k
    v_off = k_off + d_gla_k
    r_off = v_off + d_gla_v
    a_off = r_off + d_gla_v
    g_off = a_off + rank
    m = batch * seq

    xs = x.reshape(m, d_model)
    for l in range(depth):
        xs = _ffn(xs, ffn1_norm_g[l][None], ffn1_w_gate[l].astype(BF16), ffn1_w_up[l].astype(BF16),
                  ffn1_w_down[l].astype(BF16))

        h = _norm(xs, mix_norm_g[l][None], BF16)
        w_l = w_in[l]
        w_main = w_l[:, :a_off].astype(BF16)
        w_alow = jnp.pad(w_l[:, a_off:g_off], ((0, 0), (0, LANES - rank))).astype(BF16)
        w_gate = w_l[:, g_off:].astype(BF16)
        proj = _matmul(h, w_main, BF16, bn=1024, name="in_proj_main")
        a_low = _matmul(h, w_alow, F32, name="in_proj_alow")
        gates = _matmul(h, w_gate, BF16, bn=1024, name="in_proj_gates")

        proj3 = proj.reshape(batch, seq, a_off)
        ua = _conv_branch(proj3, conv_w[l], conv_b[l][None], d_conv)
        w_up_pad = jnp.pad(w_alpha_up[l], ((0, LANES - rank), (0, 0)))
        ub = _gla(proj3, a_low.reshape(batch, seq, LANES), w_up_pad, b_alpha[l][None], gla_norm_g[l][None],
                  q_off=q_off, k_off=k_off, v_off=v_off, r_off=r_off, hk=hk, hv=hv)
        merged = _merge(ua.reshape(m, d_conv), ub.reshape(m, d_gla_v), w_conv_out[l].astype(BF16),
                        w_gla_out[l].astype(BF16), gates, b_merge[l])
        xs = _matmul(merged, w_mix_out[l].astype(BF16), F32, res=xs, name="mix_out")

        xs = _ffn(xs, ffn2_norm_g[l][None], ffn2_w_gate[l].astype(BF16), ffn2_w_up[l].astype(BF16),
                  ffn2_w_down[l].astype(BF16))
    out = _norm(xs, final_norm_g[None], F32)
    return out.reshape(batch, seq, d_model)
```

```python
import functools

import jax
import jax.numpy as jnp
import numpy as np
from jax import lax
from jax.experimental import pallas as pl
from jax.experimental.pallas import tpu as pltpu

F32 = jnp.float32
BF16 = jnp.bfloat16

EPS = 1e-6
FFN_RES = 0.5
CHUNK = 64
CONV_W = 3
GLA_HEADS = 4
GATE_TEMP = 16.0
N_LEVELS = 6
FFN_DOWN_COLS = 1024

V7X_VMEM_LIMIT_BYTES = 58 * 1024 * 1024
LANES = 128


def _params(*sem):
    return pltpu.CompilerParams(dimension_semantics=sem, vmem_limit_bytes=V7X_VMEM_LIMIT_BYTES)


def _rms(x, g):
    ms = jnp.mean(x * x, axis=-1, keepdims=True)
    return x * lax.rsqrt(ms + EPS) * g


def _split_bf16(x):
    hi = x.astype(BF16)
    lo = (x - hi.astype(F32)).astype(BF16)
    return hi, lo


def _ffn_body(x_ref, g_ref, wg_ref, wu_ref, wd_ref, o_ref, h_ref):
    f = pl.program_id(1)

    @pl.when(f == 0)
    def _():
        h_ref[...] = _rms(x_ref[...], g_ref[...]).astype(BF16)
        o_ref[...] = jnp.zeros_like(o_ref)

    h = h_ref[...]
    gate = jnp.dot(h, wg_ref[...], preferred_element_type=F32)
    up = jnp.dot(h, wu_ref[...], preferred_element_type=F32)
    act = (gate * jax.nn.sigmoid(gate) * up).astype(BF16)
    d = o_ref.shape[1]
    for n0 in range(0, d, FFN_DOWN_COLS):
        sl = slice(n0, n0 + FFN_DOWN_COLS)
        o_ref[:, sl] += jnp.dot(act, wd_ref[:, sl], preferred_element_type=F32)

    @pl.when(f == pl.num_programs(1) - 1)
    def _():
        o_ref[...] = x_ref[...] + FFN_RES * o_ref[...]


def _ffn(x, g, wg, wu, wd, *, bm=512, bf=256):
    m, d = x.shape
    dff = wg.shape[1]
    return pl.pallas_call(
        _ffn_body,
        grid=(m // bm, dff // bf),
        in_specs=[
            pl.BlockSpec((bm, d), lambda i, f: (i, 0)),
            pl.BlockSpec((1, d), lambda i, f: (0, 0)),
            pl.BlockSpec((d, bf), lambda i, f: (0, f)),
            pl.BlockSpec((d, bf), lambda i, f: (0, f)),
            pl.BlockSpec((bf, d), lambda i, f: (f, 0)),
        ],
        out_specs=pl.BlockSpec((bm, d), lambda i, f: (i, 0)),
        out_shape=jax.ShapeDtypeStruct((m, d), F32),
        scratch_shapes=[pltpu.VMEM((bm, d), BF16)],
        compiler_params=_params("parallel", "arbitrary"),
        name="ffn",
    )(x, g, wg, wu, wd)


def _norm_body(x_ref, g_ref, o_ref):
    o_ref[...] = _rms(x_ref[...], g_ref[...]).astype(o_ref.dtype)


def _norm(x, g, out_dtype, *, bm=512):
    m, d = x.shape
    return pl.pallas_call(
        _norm_body,
        grid=(m // bm,),
        in_specs=[pl.BlockSpec((bm, d), lambda i: (i, 0)), pl.BlockSpec((1, d), lambda i: (0, 0))],
        out_specs=pl.BlockSpec((bm, d), lambda i: (i, 0)),
        out_shape=jax.ShapeDtypeStruct((m, d), out_dtype),
        compiler_params=_params("parallel"),
        name="norm",
    )(x, g)


def _mm_body(a_ref, w_ref, o_ref):
    o_ref[...] = jnp.dot(a_ref[...], w_ref[...], preferred_element_type=F32).astype(o_ref.dtype)


def _mm_res_body(a_ref, w_ref, r_ref, o_ref):
    acc = jnp.dot(a_ref[...], w_ref[...], preferred_element_type=F32)
    o_ref[...] = (r_ref[...] + acc).astype(o_ref.dtype)


def _matmul(a, w, out_dtype, *, res=None, bm=1024, bn=512, name="matmul"):
    m, k = a.shape
    n = w.shape[1]
    bn = min(bn, n)
    in_specs = [pl.BlockSpec((bm, k), lambda i, j: (i, 0)), pl.BlockSpec((k, bn), lambda i, j: (0, j))]
    args = [a, w]
    body = _mm_body
    if res is not None:
        in_specs.append(pl.BlockSpec((bm, bn), lambda i, j: (i, j)))
        args.append(res)
        body = _mm_res_body
    return pl.pallas_call(
        body,
        grid=(m // bm, n // bn),
        in_specs=in_specs,
        out_specs=pl.BlockSpec((bm, bn), lambda i, j: (i, j)),
        out_shape=jax.ShapeDtypeStruct((m, n), out_dtype),
        compiler_params=_params("parallel", "arbitrary"),
        name=name,
    )(*args)


def _mm_wcast_body(a_ref, w_ref, o_ref, wbf_ref):
    @pl.when(pl.program_id(1) == 0)
    def _():
        wbf_ref[...] = w_ref[0].astype(BF16)

    o_ref[...] = jnp.dot(a_ref[...], wbf_ref[...], preferred_element_type=F32).astype(o_ref.dtype)


def _matmul_wcast(a, w3, layer, n, out_dtype, *, bm=1024, bn=512, name="matmul_wcast"):
    m, k = a.shape
    return pl.pallas_call(
        _mm_wcast_body,
        grid=(n // bn, m // bm),
        in_specs=[
            pl.BlockSpec((bm, k), lambda j, i: (i, 0)),
            pl.BlockSpec((1, k, bn), lambda j, i: (layer, 0, j)),
        ],
        out_specs=pl.BlockSpec((bm, bn), lambda j, i: (i, j)),
        out_shape=jax.ShapeDtypeStruct((m, n), out_dtype),
        scratch_shapes=[pltpu.VMEM((k, bn), BF16)],
        compiler_params=_params("parallel", "arbitrary"),
        name=name,
    )(a, w3)


def _conv_body(cb_ref, cc_ref, cu_ref, w_ref, b_ref, o_ref):
    p = cc_ref[0].astype(F32) * cu_ref[0].astype(F32)
    row = lax.broadcasted_iota(jnp.int32, p.shape, 0)
    p1 = jnp.where(row >= 1, pltpu.roll(p, 1, axis=0), 0.0)
    p2 = jnp.where(row >= 2, pltpu.roll(p, 2, axis=0), 0.0)
    w = w_ref[...]
    conv = b_ref[...] + w[0:1, :] * p2
    conv = conv + w[1:2, :] * p1
    conv = conv + w[2:3, :] * p
    o_ref[0] = (cb_ref[0].astype(F32) * conv).astype(o_ref.dtype)


def _conv_branch(proj3, conv_w, conv_b, d_conv, *, bc=256):
    b, s, _ = proj3.shape
    nb = d_conv // bc
    return pl.pallas_call(
        _conv_body,
        grid=(b, nb),
        in_specs=[
            pl.BlockSpec((1, s, bc), lambda i, j: (i, 0, j)),
            pl.BlockSpec((1, s, bc), lambda i, j: (i, 0, nb + j)),
            pl.BlockSpec((1, s, bc), lambda i, j: (i, 0, 2 * nb + j)),
            pl.BlockSpec((CONV_W, bc), lambda i, j: (0, j)),
            pl.BlockSpec((1, bc), lambda i, j: (0, j)),
        ],
        out_specs=pl.BlockSpec((1, s, bc), lambda i, j: (i, 0, j)),
        out_shape=jax.ShapeDtypeStruct((b, s, d_conv), BF16),
        compiler_params=_params("parallel", "parallel"),
        name="conv",
    )(proj3, proj3, proj3, conv_w, conv_b)


def _loggate_body(h_ref, wl_ref, wup_ref, ba_ref, o_ref):
    a = jnp.dot(h_ref[...], wl_ref[...], preferred_element_type=F32)
    a_hi, a_lo = _split_bf16(a)
    w_hi, w_lo = _split_bf16(wup_ref[...])
    z = (jnp.dot(a_hi, w_hi, preferred_element_type=F32) + jnp.dot(a_hi, w_lo, preferred_element_type=F32)
         + jnp.dot(a_lo, w_hi, preferred_element_type=F32)) + ba_ref[...]
    o_ref[...] = (jnp.minimum(z, 0.0) - jnp.log1p(jnp.exp(-jnp.abs(z)))) / GATE_TEMP


def _loggate(h, w_low, w_up, b_alpha, *, bm=1024):
    m, d = h.shape
    rank_pad, dk = w_up.shape
    return pl.pallas_call(
        _loggate_body,
        grid=(m // bm,),
        in_specs=[
            pl.BlockSpec((bm, d), lambda i: (i, 0)),
            pl.BlockSpec((d, rank_pad), lambda i: (0, 0)),
            pl.BlockSpec((rank_pad, dk), lambda i: (0, 0)),
            pl.BlockSpec((1, dk), lambda i: (0, 0)),
        ],
        out_specs=pl.BlockSpec((bm, dk), lambda i: (i, 0)),
        out_shape=jax.ShapeDtypeStruct((m, dk), F32),
        compiler_params=_params("parallel"),
        name="loggate",
    )(h, w_low, w_up, b_alpha)


def _gla_tables():
    c = CHUNK
    i = np.arange(c)[:, None]
    t = np.arange(c)[None, :]
    blocks = [(t <= i), (t > i)]
    masks = []
    for lvl in range(N_LEVELS):
        p = c >> lvl
        half = p // 2
        mid = (i // p) * p + half
        upper = i >= mid
        blocks.append(np.where(upper, (t > mid) & (t <= i), (t > i) & (t <= mid)))
        j = t
        same_parent = (i // p) == (j // p)
        masks.append(same_parent & (((i % p) >= half) != ((j % p) >= half)))
    expo = np.concatenate(blocks, axis=0).astype(np.float32)
    mask = np.stack(masks).astype(np.float32)
    return expo, mask


def _gla_body(q_ref, k_ref, v_ref, r_ref, la_ref, gn_ref, expo_ref, mask_ref, o_ref, state_ref, *, scale, hk, hv):
    c = CHUNK
    nt = (((1,), (1,)), ((), ()))
    tn = (((0,), (0,)), ((), ()))

    @pl.when(pl.program_id(1) == 0)
    def _():
        state_ref[...] = jnp.zeros_like(state_ref)

    row = lax.broadcasted_iota(jnp.int32, (c, c), 0)
    col = lax.broadcasted_iota(jnp.int32, (c, c), 1)
    eye = row == col
    for h in range(GLA_HEADS):
        ks = slice(h * hk, (h + 1) * hk)
        vs = slice(h * hv, (h + 1) * hv)
        q = q_ref[0, :, ks].astype(F32) * scale
        k = k_ref[0, :, ks].astype(F32)
        v = v_ref[0, :, vs]

        hi, lo = _split_bf16(la_ref[0, :, ks])
        expo = jnp.dot(expo_ref[...], jnp.concatenate([hi, lo], axis=0), preferred_element_type=F32)
        e_all = jnp.exp(expo)
        e_l = e_all[0:c]
        e_rest = e_all[c:2 * c]

        scores = jnp.where(eye, jnp.sum(q * k, axis=-1, keepdims=True), 0.0)
        for lvl in range(N_LEVELS):
            e = e_all[(2 + lvl) * c:(3 + lvl) * c]
            p = lax.dot_general((q * e).astype(BF16), (k * e).astype(BF16), nt, preferred_element_type=F32)
            scores = scores + mask_ref[lvl] * p

        state = state_ref[h]
        o = jnp.dot(scores.astype(BF16), v, preferred_element_type=F32)
        o = o + lax.dot_general((q * e_l).astype(BF16), state.astype(BF16), nt, preferred_element_type=F32)

        k_dec = (k * e_rest).astype(BF16)
        state_ref[h] = state * e_l[c - 1:c, :] + lax.dot_general(v, k_dec, tn, preferred_element_type=F32)

        r = r_ref[0, :, vs].astype(F32)
        o_ref[0, :, vs] = (_rms(o, gn_ref[...]) * (r * jax.nn.sigmoid(r))).astype(o_ref.dtype)


def _gla(proj3, la3, gn, *, q_off, k_off, v_off, r_off, hk, hv):
    b, s, _ = proj3.shape
    nc = s // CHUNK
    dk, dv = GLA_HEADS * hk, GLA_HEADS * hv
    expo, mask = _gla_tables()
    expo2 = np.concatenate([expo, expo], axis=1)
    body = functools.partial(_gla_body, scale=float(hk) ** -0.5, hk=hk, hv=hv)
    return pl.pallas_call(
        body,
        grid=(b, nc),
        in_specs=[
            pl.BlockSpec((1, CHUNK, dk), lambda i, c: (i, c, q_off // dk)),
            pl.BlockSpec((1, CHUNK, dk), lambda i, c: (i, c, k_off // dk)),
            pl.BlockSpec((1, CHUNK, dv), lambda i, c: (i, c, v_off // dv)),
            pl.BlockSpec((1, CHUNK, dv), lambda i, c: (i, c, r_off // dv)),
            pl.BlockSpec((1, CHUNK, dk), lambda i, c: (i, c, 0)),
            pl.BlockSpec((1, hv), lambda i, c: (0, 0)),
            pl.BlockSpec(expo2.shape, lambda i, c: (0, 0)),
            pl.BlockSpec(mask.shape, lambda i, c: (0, 0, 0)),
        ],
        out_specs=pl.BlockSpec((1, CHUNK, dv), lambda i, c: (i, c, 0)),
        out_shape=jax.ShapeDtypeStruct((b, s, dv), BF16),
        scratch_shapes=[pltpu.VMEM((GLA_HEADS, hv, hk), F32)],
        compiler_params=_params("parallel", "arbitrary"),
        name="gla",
    )(proj3, proj3, proj3, proj3, la3, gn, jnp.asarray(expo2, dtype=BF16), jnp.asarray(mask))


def _merge_body(ua_ref, ub_ref, wa_ref, wb_ref, ga_ref, gb_ref, bm_ref, o_ref):
    ya = jnp.dot(ua_ref[...], wa_ref[...], preferred_element_type=F32)
    yb = jnp.dot(ub_ref[...], wb_ref[...], preferred_element_type=F32)
    bias = bm_ref[...]
    sa = jax.nn.sigmoid(ga_ref[...].astype(F32) + bias[0:1, :])
    sb = jax.nn.sigmoid(gb_ref[...].astype(F32) + bias[1:2, :])
    o_ref[...] = (sa * ya + sb * yb).astype(o_ref.dtype)


def _merge(ua, ub, wa, wb, gates, b_merge, *, bm=1024, bn=512):
    m, kk = ua.shape
    n = wa.shape[1]
    nb = n // bn
    return pl.pallas_call(
        _merge_body,
        grid=(m // bm, nb),
        in_specs=[
            pl.BlockSpec((bm, kk), lambda i, j: (i, 0)),
            pl.BlockSpec((bm, kk), lambda i, j: (i, 0)),
            pl.BlockSpec((kk, bn), lambda i, j: (0, j)),
            pl.BlockSpec((kk, bn), lambda i, j: (0, j)),
            pl.BlockSpec((bm, bn), lambda i, j: (i, j)),
            pl.BlockSpec((bm, bn), lambda i, j: (i, nb + j)),
            pl.BlockSpec((2, bn), lambda i, j: (0, j)),
        ],
        out_specs=pl.BlockSpec((bm, bn), lambda i, j: (i, j)),
        out_shape=jax.ShapeDtypeStruct((m, n), BF16),
        compiler_params=_params("parallel", "arbitrary"),
        name="merge",
    )(ua, ub, wa, wb, gates, gates, b_merge)


def kernel(x, ffn1_norm_g, ffn1_w_gate, ffn1_w_up, ffn1_w_down, mix_norm_g, w_in, conv_w, conv_b, w_conv_out,
           w_alpha_up, b_alpha, gla_norm_g, w_gla_out, b_merge, w_mix_out, ffn2_norm_g, ffn2_w_gate, ffn2_w_up,
           ffn2_w_down, final_norm_g):
    batch, seq, d_model = x.shape
    depth = ffn1_w_gate.shape[0]
    d_conv = conv_w.shape[-1]
    rank, d_gla_k = w_alpha_up.shape[-2:]
    d_gla_v = w_gla_out.shape[-2]
    hk, hv = d_gla_k // GLA_HEADS, d_gla_v // GLA_HEADS
    q_off = 3 * d_conv
    k_off = q_off + d_gla_k
    v_off = k_off + d_gla_k
    r_off = v_off + d_gla_v
    a_off = r_off + d_gla_v
    g_off = a_off + rank
    m = batch * seq

    xs = x.reshape(m, d_model)
    for l in range(depth):
        xs = _ffn(xs, ffn1_norm_g[l][None], ffn1_w_gate[l].astype(BF16), ffn1_w_up[l].astype(BF16),
                  ffn1_w_down[l].astype(BF16))

        h = _norm(xs, mix_norm_g[l][None], BF16)
        proj = _matmul_wcast(h, w_in, l, a_off, BF16, name="in_proj_main")
        w_alow = jnp.pad(w_in[l, :, a_off:g_off], ((0, 0), (0, LANES - rank))).astype(BF16)
        w_gate = w_in[l, :, g_off:].astype(BF16)
        w_up_pad = jnp.pad(w_alpha_up[l], ((0, LANES - rank), (0, 0)))
        la = _loggate(h, w_alow, w_up_pad, b_alpha[l][None])
        gates = _matmul(h, w_gate, BF16, bn=1024, name="in_proj_gates")

        proj3 = proj.reshape(batch, seq, a_off)
        ua = _conv_branch(proj3, conv_w[l], conv_b[l][None], d_conv)
        ub = _gla(proj3, la.reshape(batch, seq, d_gla_k), gla_norm_g[l][None],
                  q_off=q_off, k_off=k_off, v_off=v_off, r_off=r_off, hk=hk, hv=hv)
        merged = _merge(ua.reshape(m, d_conv), ub.reshape(m, d_gla_v), w_conv_out[l].astype(BF16),
                        w_gla_out[l].astype(BF16), gates, b_merge[l])
        xs = _matmul(merged, w_mix_out[l].astype(BF16), F32, res=xs, name="mix_out")

        xs = _ffn(xs, ffn2_norm_g[l][None], ffn2_w_gate[l].astype(BF16), ffn2_w_up[l].astype(BF16),
                  ffn2_w_down[l].astype(BF16))
    out = _norm(xs, final_norm_g[None], F32)
    return out.reshape(batch, seq, d_model)
```

```python
import functools

import jax
import jax.numpy as jnp
import numpy as np
from jax import lax
from jax.experimental import pallas as pl
from jax.experimental.pallas import tpu as pltpu

F32 = jnp.float32
BF16 = jnp.bfloat16

EPS = 1e-6
FFN_RES = 0.5
CHUNK = 64
CONV_W = 3
GLA_HEADS = 4
GATE_TEMP = 16.0
N_LEVELS = 6
FFN_DOWN_COLS = 1024

V7X_VMEM_LIMIT_BYTES = 58 * 1024 * 1024
LANES = 128
SUBLANES = 8


def _params(*sem):
    return pltpu.CompilerParams(dimension_semantics=sem, vmem_limit_bytes=V7X_VMEM_LIMIT_BYTES)


def _rms(x, g):
    ms = jnp.mean(x * x, axis=-1, keepdims=True)
    return x * lax.rsqrt(ms + EPS) * g


def _split_bf16(x):
    hi = x.astype(BF16)
    lo = (x - hi.astype(F32)).astype(BF16)
    return hi, lo


def _ffn_body(x_ref, g_ref, wg_ref, wu_ref, wd_ref, o_ref, h_ref):
    f = pl.program_id(1)

    @pl.when(f == 0)
    def _():
        h_ref[...] = _rms(x_ref[...], g_ref[...]).astype(BF16)
        o_ref[...] = jnp.zeros_like(o_ref)

    h = h_ref[...]
    gate = jnp.dot(h, wg_ref[...], preferred_element_type=F32)
    up = jnp.dot(h, wu_ref[...], preferred_element_type=F32)
    act = (gate * jax.nn.sigmoid(gate) * up).astype(BF16)
    d = o_ref.shape[1]
    for n0 in range(0, d, FFN_DOWN_COLS):
        sl = slice(n0, n0 + FFN_DOWN_COLS)
        o_ref[:, sl] += jnp.dot(act, wd_ref[:, sl], preferred_element_type=F32)

    @pl.when(f == pl.num_programs(1) - 1)
    def _():
        o_ref[...] = x_ref[...] + FFN_RES * o_ref[...]


def _ffn(x, g, wg, wu, wd, *, bm=512, bf=256):
    m, d = x.shape
    dff = wg.shape[1]
    return pl.pallas_call(
        _ffn_body,
        grid=(m // bm, dff // bf),
        in_specs=[
            pl.BlockSpec((bm, d), lambda i, f: (i, 0)),
            pl.BlockSpec((1, d), lambda i, f: (0, 0)),
            pl.BlockSpec((d, bf), lambda i, f: (0, f)),
            pl.BlockSpec((d, bf), lambda i, f: (0, f)),
            pl.BlockSpec((bf, d), lambda i, f: (f, 0)),
        ],
        out_specs=pl.BlockSpec((bm, d), lambda i, f: (i, 0)),
        out_shape=jax.ShapeDtypeStruct((m, d), F32),
        scratch_shapes=[pltpu.VMEM((bm, d), BF16)],
        compiler_params=_params("parallel", "arbitrary"),
        name="ffn",
    )(x, g, wg, wu, wd)


def _norm_body(x_ref, g_ref, o_ref):
    o_ref[...] = _rms(x_ref[...], g_ref[...]).astype(o_ref.dtype)


def _norm(x, g, out_dtype, *, bm=512):
    m, d = x.shape
    return pl.pallas_call(
        _norm_body,
        grid=(m // bm,),
        in_specs=[pl.BlockSpec((bm, d), lambda i: (i, 0)), pl.BlockSpec((1, d), lambda i: (0, 0))],
        out_specs=pl.BlockSpec((bm, d), lambda i: (i, 0)),
        out_shape=jax.ShapeDtypeStruct((m, d), out_dtype),
        compiler_params=_params("parallel"),
        name="norm",
    )(x, g)


def _mm_body(a_ref, w_ref, o_ref):
    o_ref[...] = jnp.dot(a_ref[...], w_ref[...], preferred_element_type=F32).astype(o_ref.dtype)


def _mm_res_body(a_ref, w_ref, r_ref, o_ref):
    acc = jnp.dot(a_ref[...], w_ref[...], preferred_element_type=F32)
    o_ref[...] = (r_ref[...] + acc).astype(o_ref.dtype)


def _matmul(a, w, out_dtype, *, res=None, bm=1024, bn=512, name="matmul"):
    m, k = a.shape
    n = w.shape[1]
    bn = min(bn, n)
    in_specs = [pl.BlockSpec((bm, k), lambda i, j: (i, 0)), pl.BlockSpec((k, bn), lambda i, j: (0, j))]
    args = [a, w]
    body = _mm_body
    if res is not None:
        in_specs.append(pl.BlockSpec((bm, bn), lambda i, j: (i, j)))
        args.append(res)
        body = _mm_res_body
    return pl.pallas_call(
        body,
        grid=(m // bm, n // bn),
        in_specs=in_specs,
        out_specs=pl.BlockSpec((bm, bn), lambda i, j: (i, j)),
        out_shape=jax.ShapeDtypeStruct((m, n), out_dtype),
        compiler_params=_params("parallel", "arbitrary"),
        name=name,
    )(*args)


def _mm_wt_body(a_ref, wt_ref, o_ref, wbf_ref):
    @pl.when(pl.program_id(1) == 0)
    def _():
        wbf_ref[...] = wt_ref[...].T.astype(BF16)

    o_ref[...] = jnp.dot(a_ref[...], wbf_ref[...], preferred_element_type=F32).astype(o_ref.dtype)


def _matmul_wt(a, wt, row_off, n, out_dtype, *, bm=1024, bn=512, name="matmul_wt"):
    m, k = a.shape
    assert row_off % SUBLANES == 0 and bn % SUBLANES == 0
    return pl.pallas_call(
        _mm_wt_body,
        grid=(n // bn, m // bm),
        in_specs=[
            pl.BlockSpec((bm, k), lambda j, i: (i, 0)),
            pl.BlockSpec((pl.Element(bn), pl.Element(k)),
                         lambda j, i: (pl.multiple_of(row_off + j * bn, SUBLANES), 0)),
        ],
        out_specs=pl.BlockSpec((bm, bn), lambda j, i: (i, j)),
        out_shape=jax.ShapeDtypeStruct((m, n), out_dtype),
        scratch_shapes=[pltpu.VMEM((k, bn), BF16)],
        compiler_params=_params("parallel", "arbitrary"),
        name=name,
    )(a, wt)


def _conv_body(cb_ref, cc_ref, cu_ref, w_ref, b_ref, o_ref):
    p = cc_ref[0].astype(F32) * cu_ref[0].astype(F32)
    row = lax.broadcasted_iota(jnp.int32, p.shape, 0)
    p1 = jnp.where(row >= 1, pltpu.roll(p, 1, axis=0), 0.0)
    p2 = jnp.where(row >= 2, pltpu.roll(p, 2, axis=0), 0.0)
    w = w_ref[...]
    conv = b_ref[...] + w[0:1, :] * p2
    conv = conv + w[1:2, :] * p1
    conv = conv + w[2:3, :] * p
    o_ref[0] = (cb_ref[0].astype(F32) * conv).astype(o_ref.dtype)


def _conv_branch(proj3, conv_w, conv_b, d_conv, *, bc=256):
    b, s, _ = proj3.shape
    nb = d_conv // bc
    return pl.pallas_call(
        _conv_body,
        grid=(b, nb),
        in_specs=[
            pl.BlockSpec((1, s, bc), lambda i, j: (i, 0, j)),
            pl.BlockSpec((1, s, bc), lambda i, j: (i, 0, nb + j)),
            pl.BlockSpec((1, s, bc), lambda i, j: (i, 0, 2 * nb + j)),
            pl.BlockSpec((CONV_W, bc), lambda i, j: (0, j)),
            pl.BlockSpec((1, bc), lambda i, j: (0, j)),
        ],
        out_specs=pl.BlockSpec((1, s, bc), lambda i, j: (i, 0, j)),
        out_shape=jax.ShapeDtypeStruct((b, s, d_conv), BF16),
        compiler_params=_params("parallel", "parallel"),
        name="conv",
    )(proj3, proj3, proj3, conv_w, conv_b)


def _loggate_body(h_ref, wl_ref, wup_ref, ba_ref, o_ref):
    nt = (((1,), (1,)), ((), ()))
    a = lax.dot_general(h_ref[...], wl_ref[...].astype(BF16), nt, preferred_element_type=F32)
    a_hi, a_lo = _split_bf16(a)
    w_hi, w_lo = _split_bf16(wup_ref[...])
    z = (jnp.dot(a_hi, w_hi, preferred_element_type=F32) + jnp.dot(a_hi, w_lo, preferred_element_type=F32)
         + jnp.dot(a_lo, w_hi, preferred_element_type=F32)) + ba_ref[...]
    o_ref[...] = (jnp.minimum(z, 0.0) - jnp.log1p(jnp.exp(-jnp.abs(z)))) / GATE_TEMP


def _loggate(h, wt, row_off, w_up, b_alpha, *, bm=1024):
    m, d = h.shape
    rank, dk = w_up.shape
    return pl.pallas_call(
        _loggate_body,
        grid=(m // bm,),
        in_specs=[
            pl.BlockSpec((bm, d), lambda i: (i, 0)),
            pl.BlockSpec((pl.Element(rank), pl.Element(d)), lambda i: (row_off, 0)),
            pl.BlockSpec((rank, dk), lambda i: (0, 0)),
            pl.BlockSpec((1, dk), lambda i: (0, 0)),
        ],
        out_specs=pl.BlockSpec((bm, dk), lambda i: (i, 0)),
        out_shape=jax.ShapeDtypeStruct((m, dk), F32),
        compiler_params=_params("parallel"),
        name="loggate",
    )(h, wt, w_up, b_alpha)


def _gla_tables():
    c = CHUNK
    i = np.arange(c)[:, None]
    t = np.arange(c)[None, :]
    blocks = [(t <= i), (t > i)]
    masks = []
    for lvl in range(N_LEVELS):
        p = c >> lvl
        half = p // 2
        mid = (i // p) * p + half
        upper = i >= mid
        blocks.append(np.where(upper, (t > mid) & (t <= i), (t > i) & (t <= mid)))
        j = t
        same_parent = (i // p) == (j // p)
        masks.append(same_parent & (((i % p) >= half) != ((j % p) >= half)))
    expo = np.concatenate(blocks, axis=0).astype(np.float32)
    mask = np.stack(masks).astype(np.float32)
    return expo, mask


def _gla_body(q_ref, k_ref, v_ref, r_ref, la_ref, gn_ref, expo_ref, mask_ref, o_ref, state_ref, *, scale, hk, hv):
    c = CHUNK
    nt = (((1,), (1,)), ((), ()))
    tn = (((0,), (0,)), ((), ()))

    @pl.when(pl.program_id(1) == 0)
    def _():
        state_ref[...] = jnp.zeros_like(state_ref)

    row = lax.broadcasted_iota(jnp.int32, (c, c), 0)
    col = lax.broadcasted_iota(jnp.int32, (c, c), 1)
    eye = row == col
    for h in range(GLA_HEADS):
        ks = slice(h * hk, (h + 1) * hk)
        vs = slice(h * hv, (h + 1) * hv)
        q = q_ref[0, :, ks].astype(F32) * scale
        k = k_ref[0, :, ks].astype(F32)
        v = v_ref[0, :, vs]

        hi, lo = _split_bf16(la_ref[0, :, ks])
        expo = jnp.dot(expo_ref[...], jnp.concatenate([hi, lo], axis=0), preferred_element_type=F32)
        e_all = jnp.exp(expo)
        e_l = e_all[0:c]
        e_rest = e_all[c:2 * c]

        scores = jnp.where(eye, jnp.sum(q * k, axis=-1, keepdims=True), 0.0)
        for lvl in range(N_LEVELS):
            e = e_all[(2 + lvl) * c:(3 + lvl) * c]
            p = lax.dot_general((q * e).astype(BF16), (k * e).astype(BF16), nt, preferred_element_type=F32)
            scores = scores + mask_ref[lvl] * p

        state = state_ref[h]
        o = jnp.dot(scores.astype(BF16), v, preferred_element_type=F32)
        o = o + lax.dot_general((q * e_l).astype(BF16), state.astype(BF16), nt, preferred_element_type=F32)

        k_dec = (k * e_rest).astype(BF16)
        state_ref[h] = state * e_l[c - 1:c, :] + lax.dot_general(v, k_dec, tn, preferred_element_type=F32)

        r = r_ref[0, :, vs].astype(F32)
        o_ref[0, :, vs] = (_rms(o, gn_ref[...]) * (r * jax.nn.sigmoid(r))).astype(o_ref.dtype)


def _gla(proj3, la3, gn, *, q_off, k_off, v_off, r_off, hk, hv):
    b, s, _ = proj3.shape
    nc = s // CHUNK
    dk, dv = GLA_HEADS * hk, GLA_HEADS * hv
    expo, mask = _gla_tables()
    expo2 = np.concatenate([expo, expo], axis=1)
    body = functools.partial(_gla_body, scale=float(hk) ** -0.5, hk=hk, hv=hv)
    return pl.pallas_call(
        body,
        grid=(b, nc),
        in_specs=[
            pl.BlockSpec((1, CHUNK, dk), lambda i, c: (i, c, q_off // dk)),
            pl.BlockSpec((1, CHUNK, dk), lambda i, c: (i, c, k_off // dk)),
            pl.BlockSpec((1, CHUNK, dv), lambda i, c: (i, c, v_off // dv)),
            pl.BlockSpec((1, CHUNK, dv), lambda i, c: (i, c, r_off // dv)),
            pl.BlockSpec((1, CHUNK, dk), lambda i, c: (i, c, 0)),
            pl.BlockSpec((1, hv), lambda i, c: (0, 0)),
            pl.BlockSpec(expo2.shape, lambda i, c: (0, 0)),
            pl.BlockSpec(mask.shape, lambda i, c: (0, 0, 0)),
        ],
        out_specs=pl.BlockSpec((1, CHUNK, dv), lambda i, c: (i, c, 0)),
        out_shape=jax.ShapeDtypeStruct((b, s, dv), BF16),
        scratch_shapes=[pltpu.VMEM((GLA_HEADS, hv, hk), F32)],
        compiler_params=_params("parallel", "arbitrary"),
        name="gla",
    )(proj3, proj3, proj3, proj3, la3, gn, jnp.asarray(expo2, dtype=BF16), jnp.asarray(mask))


def _merge_body(ua_ref, ub_ref, wa_ref, wb_ref, ga_ref, gb_ref, bm_ref, o_ref):
    ya = jnp.dot(ua_ref[...], wa_ref[...], preferred_element_type=F32)
    yb = jnp.dot(ub_ref[...], wb_ref[...], preferred_element_type=F32)
    bias = bm_ref[...]
    sa = jax.nn.sigmoid(ga_ref[...].astype(F32) + bias[0:1, :])
    sb = jax.nn.sigmoid(gb_ref[...].astype(F32) + bias[1:2, :])
    o_ref[...] = (sa * ya + sb * yb).astype(o_ref.dtype)


def _merge(ua, ub, wa, wb, gates, b_merge, *, bm=1024, bn=512):
    m, kk = ua.shape
    n = wa.shape[1]
    nb = n // bn
    return pl.pallas_call(
        _merge_body,
        grid=(m // bm, nb),
        in_specs=[
            pl.BlockSpec((bm, kk), lambda i, j: (i, 0)),
            pl.BlockSpec((bm, kk), lambda i, j: (i, 0)),
            pl.BlockSpec((kk, bn), lambda i, j: (0, j)),
            pl.BlockSpec((kk, bn), lambda i, j: (0, j)),
            pl.BlockSpec((bm, bn), lambda i, j: (i, j)),
            pl.BlockSpec((bm, bn), lambda i, j: (i, nb + j)),
            pl.BlockSpec((2, bn), lambda i, j: (0, j)),
        ],
        out_specs=pl.BlockSpec((bm, bn), lambda i, j: (i, j)),
        out_shape=jax.ShapeDtypeStruct((m, n), BF16),
        compiler_params=_params("parallel", "arbitrary"),
        name="merge",
    )(ua, ub, wa, wb, gates, gates, b_merge)


def kernel(x, ffn1_norm_g, ffn1_w_gate, ffn1_w_up, ffn1_w_down, mix_norm_g, w_in, conv_w, conv_b, w_conv_out,
           w_alpha_up, b_alpha, gla_norm_g, w_gla_out, b_merge, w_mix_out, ffn2_norm_g, ffn2_w_gate, ffn2_w_up,
           ffn2_w_down, final_norm_g):
    batch, seq, d_model = x.shape
    depth = ffn1_w_gate.shape[0]
    d_conv = conv_w.shape[-1]
    rank, d_gla_k = w_alpha_up.shape[-2:]
    d_gla_v = w_gla_out.shape[-2]
    hk, hv = d_gla_k // GLA_HEADS, d_gla_v // GLA_HEADS
    q_off = 3 * d_conv
    k_off = q_off + d_gla_k
    v_off = k_off + d_gla_k
    r_off = v_off + d_gla_v
    a_off = r_off + d_gla_v
    g_off = a_off + rank
    m = batch * seq

    xs = x.reshape(m, d_model)
    for l in range(depth):
        xs = _ffn(xs, ffn1_norm_g[l][None], ffn1_w_gate[l].astype(BF16), ffn1_w_up[l].astype(BF16),
                  ffn1_w_down[l].astype(BF16))

        h = _norm(xs, mix_norm_g[l][None], BF16)
        wt = jnp.transpose(w_in[l])
        proj = _matmul_wt(h, wt, 0, a_off, BF16, name="in_proj_main")
        gates = _matmul_wt(h, wt, g_off, 2 * d_model, BF16, name="in_proj_gates")
        la = _loggate(h, wt, a_off, w_alpha_up[l], b_alpha[l][None])

        proj3 = proj.reshape(batch, seq, a_off)
        ua = _conv_branch(proj3, conv_w[l], conv_b[l][None], d_conv)
        ub = _gla(proj3, la.reshape(batch, seq, d_gla_k), gla_norm_g[l][None],
                  q_off=q_off, k_off=k_off, v_off=v_off, r_off=r_off, hk=hk, hv=hv)
        merged = _merge(ua.reshape(m, d_conv), ub.reshape(m, d_gla_v), w_conv_out[l].astype(BF16),
                        w_gla_out[l].astype(BF16), gates, b_merge[l])
        xs = _matmul(merged, w_mix_out[l].astype(BF16), F32, res=xs, name="mix_out")

        xs = _ffn(xs, ffn2_norm_g[l][None], ffn2_w_gate[l].astype(BF16), ffn2_w_up[l].astype(BF16),
                  ffn2_w_down[l].astype(BF16))
    out = _norm(xs, final_norm_g[None], F32)
    return out.reshape(batch, seq, d_model)
```

```python
import functools

import jax
import jax.numpy as jnp
import numpy as np
from jax import lax
from jax.experimental import pallas as pl
from jax.experimental.pallas import tpu as pltpu

F32 = jnp.float32
BF16 = jnp.bfloat16

EPS = 1e-6
FFN_RES = 0.5
CHUNK = 64
CONV_W = 3
GLA_HEADS = 4
GATE_TEMP = 16.0
N_LEVELS = 6
FFN_DOWN_COLS = 1024
NORM_ROWS = 16
NORM_UNROLL = 4
SIDE_CAST_BLOCK = 256

V7X_VMEM_LIMIT_BYTES = 58 * 1024 * 1024
LANES = 128
SUBLANES = 8


def _params(*sem):
    return pltpu.CompilerParams(dimension_semantics=sem, vmem_limit_bytes=V7X_VMEM_LIMIT_BYTES)


def _rms(x, g):
    ms = jnp.mean(x * x, axis=-1, keepdims=True)
    return x * lax.rsqrt(ms + EPS) * g


def _split_bf16(x):
    hi = x.astype(BF16)
    lo = (x - hi.astype(F32)).astype(BF16)
    return hi, lo


def _ffn_body(x_ref, g_ref, wg_ref, wu_ref, wd_ref, gn_ref, *refs, emit):
    if emit == "x+norm":
        o_ref, hn_ref, h_ref, rs_ref = refs
    else:
        o_ref, h_ref, rs_ref = refs
    f = pl.program_id(1)
    bm, d = o_ref.shape
    n_slabs = bm // NORM_ROWS

    def rows(s):
        return pl.ds(pl.multiple_of(s * NORM_ROWS, NORM_ROWS), NORM_ROWS)

    @pl.when(f == 0)
    def _():
        g = g_ref[...]

        def slab(s, carry):
            h_ref[rows(s), :] = _rms(x_ref[rows(s), :], g).astype(BF16)
            o_ref[rows(s), :] = jnp.zeros((NORM_ROWS, d), F32)
            return carry

        lax.fori_loop(0, n_slabs, slab, 0, unroll=NORM_UNROLL)

    h = h_ref[...]
    gate = jnp.dot(h, wg_ref[...], preferred_element_type=F32)
    up = jnp.dot(h, wu_ref[...], preferred_element_type=F32)
    act = (gate * jax.nn.sigmoid(gate) * up).astype(BF16)
    for n0 in range(0, d, FFN_DOWN_COLS):
        sl = slice(n0, n0 + FFN_DOWN_COLS)
        o_ref[:, sl] += jnp.dot(act, wd_ref[:, sl], preferred_element_type=F32)

    @pl.when(f == pl.num_programs(1) - 1)
    def _():
        gn = gn_ref[...]

        def slab(s, carry):
            y = x_ref[rows(s), :] + FFN_RES * o_ref[rows(s), :]
            o_ref[rows(s), :] = y
            if emit != "x":
                rs_ref[rows(s), :] = lax.rsqrt(jnp.mean(y * y, axis=-1, keepdims=True) + EPS)
            return carry

        lax.fori_loop(0, n_slabs, slab, 0, unroll=NORM_UNROLL)

        def norm_slab(s, carry):
            normed = o_ref[rows(s), :] * rs_ref[rows(s), :] * gn
            if emit == "norm":
                o_ref[rows(s), :] = normed
            else:
                hn_ref[rows(s), :] = normed.astype(BF16)
            return carry

        if emit != "x":
            lax.fori_loop(0, n_slabs, norm_slab, 0, unroll=NORM_UNROLL)


def _ffn(x, g, wg, wu, wd, g_next, emit, *, bm=512, bf=256):
    m, d = x.shape
    dff = wg.shape[1]
    row_block = pl.BlockSpec((bm, d), lambda i, f: (i, 0))
    vec = pl.BlockSpec((1, d), lambda i, f: (0, 0))
    out_specs, out_shape = row_block, jax.ShapeDtypeStruct((m, d), F32)
    if emit == "x+norm":
        out_specs, out_shape = [row_block, row_block], [out_shape, jax.ShapeDtypeStruct((m, d), BF16)]
    return pl.pallas_call(
        functools.partial(_ffn_body, emit=emit),
        grid=(m // bm, dff // bf),
        in_specs=[
            row_block,
            vec,
            pl.BlockSpec((d, bf), lambda i, f: (0, f)),
            pl.BlockSpec((d, bf), lambda i, f: (0, f)),
            pl.BlockSpec((bf, d), lambda i, f: (f, 0)),
            vec,
        ],
        out_specs=out_specs,
        out_shape=out_shape,
        scratch_shapes=[pltpu.VMEM((bm, d), BF16), pltpu.VMEM((bm, 1), F32)],
        compiler_params=_params("parallel", "arbitrary"),
        name="ffn",
    )(x, g, wg, wu, wd, g_next)


class _SideCast:
    def __init__(self, src, block, grid):
        rows, cols = src.shape
        br, bc = block
        assert rows % br == 0 and cols % bc == 0 and (br == rows or bc == cols)
        n_blocks = (rows // br) * (cols // bc)
        assert grid[0] * grid[1] >= n_blocks
        n_inner = grid[1]

        def index(i, j):
            t = jnp.minimum(i * n_inner + j, n_blocks - 1)
            return (t, 0) if bc == cols else (0, t)

        self.src = src
        self.in_spec = pl.BlockSpec(block, index)
        self.out_spec = pl.BlockSpec(block, index)
        self.out_shape = jax.ShapeDtypeStruct(src.shape, BF16)

    @staticmethod
    def step(src_ref, dst_ref):
        dst_ref[...] = src_ref[...].astype(BF16)


def _with_side(side, in_specs, args, out_specs, out_shape):
    if side is None:
        return in_specs, args, out_specs, out_shape
    return in_specs + [side.in_spec], args + [side.src], [out_specs, side.out_spec], [out_shape, side.out_shape]


def _mm_body(*refs, has_res, has_side):
    a_ref, w_ref = refs[:2]
    refs = refs[2:]
    r_ref = None
    if has_res:
        r_ref, refs = refs[0], refs[1:]
    if has_side:
        side_src, o_ref, side_dst = refs
        _SideCast.step(side_src, side_dst)
    else:
        (o_ref,) = refs
    acc = jnp.dot(a_ref[...], w_ref[...].astype(BF16), preferred_element_type=F32)
    if has_res:
        acc = r_ref[...] + acc
    o_ref[...] = acc.astype(o_ref.dtype)


def _matmul(a, w, out_dtype, *, res=None, side=None, bm=1024, bn=512, name="matmul"):
    m, k = a.shape
    n = w.shape[1]
    bn = min(bn, n)
    grid = (m // bm, n // bn)
    in_specs = [pl.BlockSpec((bm, k), lambda i, j: (i, 0)), pl.BlockSpec((k, bn), lambda i, j: (0, j))]
    args = [a, w]
    if res is not None:
        in_specs.append(pl.BlockSpec((bm, bn), lambda i, j: (i, j)))
        args.append(res)
    side = None if side is None else _SideCast(*side, grid)
    in_specs, args, out_specs, out_shape = _with_side(
        side, in_specs, args, pl.BlockSpec((bm, bn), lambda i, j: (i, j)), jax.ShapeDtypeStruct((m, n), out_dtype))
    return pl.pallas_call(
        functools.partial(_mm_body, has_res=res is not None, has_side=side is not None),
        grid=grid,
        in_specs=in_specs,
        out_specs=out_specs,
        out_shape=out_shape,
        compiler_params=_params("arbitrary", "arbitrary"),
        name=name,
    )(*args)


def _mm_wt_body(a_ref, wt_ref, *refs, has_side):
    if has_side:
        side_src, o_ref, side_dst, wbf_ref = refs
        _SideCast.step(side_src, side_dst)
    else:
        o_ref, wbf_ref = refs

    @pl.when(pl.program_id(1) == 0)
    def _():
        wbf_ref[...] = wt_ref[...].T.astype(BF16)

    o_ref[...] = jnp.dot(a_ref[...], wbf_ref[...], preferred_element_type=F32).astype(o_ref.dtype)


def _matmul_wt(a, wt, row_off, n, out_dtype, *, side=None, bm=512, bn=1024, name="matmul_wt"):
    m, k = a.shape
    assert row_off % SUBLANES == 0 and bn % SUBLANES == 0
    grid = (n // bn, m // bm)
    in_specs = [
        pl.BlockSpec((bm, k), lambda j, i: (i, 0)),
        pl.BlockSpec((pl.Element(bn), pl.Element(k)), lambda j, i: (pl.multiple_of(row_off + j * bn, SUBLANES), 0)),
    ]
    side = None if side is None else _SideCast(*side, grid)
    in_specs, args, out_specs, out_shape = _with_side(
        side, in_specs, [a, wt], pl.BlockSpec((bm, bn), lambda j, i: (i, j)), jax.ShapeDtypeStruct((m, n), out_dtype))
    return pl.pallas_call(
        functools.partial(_mm_wt_body, has_side=side is not None),
        grid=grid,
        in_specs=in_specs,
        out_specs=out_specs,
        out_shape=out_shape,
        scratch_shapes=[pltpu.VMEM((k, bn), BF16)],
        compiler_params=_params("arbitrary", "arbitrary"),
        name=name,
    )(*args)


def _conv_body(cb_ref, cc_ref, cu_ref, w_ref, b_ref, o_ref):
    p = cc_ref[0].astype(F32) * cu_ref[0].astype(F32)
    row = lax.broadcasted_iota(jnp.int32, p.shape, 0)
    p1 = jnp.where(row >= 1, pltpu.roll(p, 1, axis=0), 0.0)
    p2 = jnp.where(row >= 2, pltpu.roll(p, 2, axis=0), 0.0)
    w = w_ref[...]
    conv = b_ref[...] + w[0:1, :] * p2
    conv = conv + w[1:2, :] * p1
    conv = conv + w[2:3, :] * p
    o_ref[0] = (cb_ref[0].astype(F32) * conv).astype(o_ref.dtype)


def _conv_branch(proj3, conv_w, conv_b, d_conv, *, bc=256):
    b, s, _ = proj3.shape
    nb = d_conv // bc
    return pl.pallas_call(
        _conv_body,
        grid=(b, nb),
        in_specs=[
            pl.BlockSpec((1, s, bc), lambda i, j: (i, 0, j)),
            pl.BlockSpec((1, s, bc), lambda i, j: (i, 0, nb + j)),
            pl.BlockSpec((1, s, bc), lambda i, j: (i, 0, 2 * nb + j)),
            pl.BlockSpec((CONV_W, bc), lambda i, j: (0, j)),
            pl.BlockSpec((1, bc), lambda i, j: (0, j)),
        ],
        out_specs=pl.BlockSpec((1, s, bc), lambda i, j: (i, 0, j)),
        out_shape=jax.ShapeDtypeStruct((b, s, d_conv), BF16),
        compiler_params=_params("parallel", "parallel"),
        name="conv",
    )(proj3, proj3, proj3, conv_w, conv_b)


def _loggate_body(h_ref, wl_ref, wup_ref, ba_ref, o_ref):
    nt = (((1,), (1,)), ((), ()))
    a = lax.dot_general(h_ref[...], wl_ref[...].astype(BF16), nt, preferred_element_type=F32)
    a_hi, a_lo = _split_bf16(a)
    w_hi, w_lo = _split_bf16(wup_ref[...])
    z = (jnp.dot(a_hi, w_hi, preferred_element_type=F32) + jnp.dot(a_hi, w_lo, preferred_element_type=F32)
         + jnp.dot(a_lo, w_hi, preferred_element_type=F32)) + ba_ref[...]
    o_ref[...] = (jnp.minimum(z, 0.0) - jnp.log1p(jnp.exp(-jnp.abs(z)))) / GATE_TEMP


def _loggate(h, wt, row_off, w_up, b_alpha, *, bm=1024):
    m, d = h.shape
    rank, dk = w_up.shape
    return pl.pallas_call(
        _loggate_body,
        grid=(m // bm,),
        in_specs=[
            pl.BlockSpec((bm, d), lambda i: (i, 0)),
            pl.BlockSpec((pl.Element(rank), pl.Element(d)), lambda i: (row_off, 0)),
            pl.BlockSpec((rank, dk), lambda i: (0, 0)),
            pl.BlockSpec((1, dk), lambda i: (0, 0)),
        ],
        out_specs=pl.BlockSpec((bm, dk), lambda i: (i, 0)),
        out_shape=jax.ShapeDtypeStruct((m, dk), F32),
        compiler_params=_params("parallel"),
        name="loggate",
    )(h, wt, w_up, b_alpha)


def _gla_tables():
    c = CHUNK
    i = np.arange(c)[:, None]
    t = np.arange(c)[None, :]
    blocks = [(t <= i), (t > i)]
    masks = []
    for lvl in range(N_LEVELS):
        p = c >> lvl
        half = p // 2
        mid = (i // p) * p + half
        upper = i >= mid
        blocks.append(np.where(upper, (t > mid) & (t <= i), (t > i) & (t <= mid)))
        j = t
        same_parent = (i // p) == (j // p)
        masks.append(same_parent & (((i % p) >= half) != ((j % p) >= half)))
    expo = np.concatenate(blocks, axis=0).astype(np.float32)
    mask = np.stack(masks).astype(np.float32)
    return expo, mask


def _gla_body(q_ref, k_ref, v_ref, r_ref, la_ref, gn_ref, expo_ref, mask_ref, o_ref, state_ref, *, scale, hk, hv):
    c = CHUNK
    nt = (((1,), (1,)), ((), ()))
    tn = (((0,), (0,)), ((), ()))

    @pl.when(pl.program_id(1) == 0)
    def _():
        state_ref[...] = jnp.zeros_like(state_ref)

    row = lax.broadcasted_iota(jnp.int32, (c, c), 0)
    col = lax.broadcasted_iota(jnp.int32, (c, c), 1)
    eye = row == col
    for h in range(GLA_HEADS):
        ks = slice(h * hk, (h + 1) * hk)
        vs = slice(h * hv, (h + 1) * hv)
        q = q_ref[0, :, ks].astype(F32) * scale
        k = k_ref[0, :, ks].astype(F32)
        v = v_ref[0, :, vs]

        hi, lo = _split_bf16(la_ref[0, :, ks])
        expo = jnp.dot(expo_ref[...], jnp.concatenate([hi, lo], axis=0), preferred_element_type=F32)
        e_all = jnp.exp(expo)
        e_l = e_all[0:c]
        e_rest = e_all[c:2 * c]

        scores = jnp.where(eye, jnp.sum(q * k, axis=-1, keepdims=True), 0.0)
        for lvl in range(N_LEVELS):
            e = e_all[(2 + lvl) * c:(3 + lvl) * c]
            p = lax.dot_general((q * e).astype(BF16), (k * e).astype(BF16), nt, preferred_element_type=F32)
            scores = scores + mask_ref[lvl] * p

        state = state_ref[h]
        o = jnp.dot(scores.astype(BF16), v, preferred_element_type=F32)
        o = o + lax.dot_general((q * e_l).astype(BF16), state.astype(BF16), nt, preferred_element_type=F32)

        k_dec = (k * e_rest).astype(BF16)
        state_ref[h] = state * e_l[c - 1:c, :] + lax.dot_general(v, k_dec, tn, preferred_element_type=F32)

        r = r_ref[0, :, vs].astype(F32)
        o_ref[0, :, vs] = (_rms(o, gn_ref[...]) * (r * jax.nn.sigmoid(r))).astype(o_ref.dtype)


def _gla(proj3, la3, gn, *, q_off, k_off, v_off, r_off, hk, hv):
    b, s, _ = proj3.shape
    nc = s // CHUNK
    dk, dv = GLA_HEADS * hk, GLA_HEADS * hv
    expo, mask = _gla_tables()
    expo2 = np.concatenate([expo, expo], axis=1)
    body = functools.partial(_gla_body, scale=float(hk) ** -0.5, hk=hk, hv=hv)
    return pl.pallas_call(
        body,
        grid=(b, nc),
        in_specs=[
            pl.BlockSpec((1, CHUNK, dk), lambda i, c: (i, c, q_off // dk)),
            pl.BlockSpec((1, CHUNK, dk), lambda i, c: (i, c, k_off // dk)),
            pl.BlockSpec((1, CHUNK, dv), lambda i, c: (i, c, v_off // dv)),
            pl.BlockSpec((1, CHUNK, dv), lambda i, c: (i, c, r_off // dv)),
            pl.BlockSpec((1, CHUNK, dk), lambda i, c: (i, c, 0)),
            pl.BlockSpec((1, hv), lambda i, c: (0, 0)),
            pl.BlockSpec(expo2.shape, lambda i, c: (0, 0)),
            pl.BlockSpec(mask.shape, lambda i, c: (0, 0, 0)),
        ],
        out_specs=pl.BlockSpec((1, CHUNK, dv), lambda i, c: (i, c, 0)),
        out_shape=jax.ShapeDtypeStruct((b, s, dv), BF16),
        scratch_shapes=[pltpu.VMEM((GLA_HEADS, hv, hk), F32)],
        compiler_params=_params("parallel", "arbitrary"),
        name="gla",
    )(proj3, proj3, proj3, proj3, la3, gn, jnp.asarray(expo2, dtype=BF16), jnp.asarray(mask))


def _merge_body(ua_ref, ub_ref, wa_ref, wb_ref, ga_ref, gb_ref, bm_ref, *refs, has_side):
    if has_side:
        side_src, o_ref, side_dst = refs
        _SideCast.step(side_src, side_dst)
    else:
        (o_ref,) = refs
    ya = jnp.dot(ua_ref[...], wa_ref[...].astype(BF16), preferred_element_type=F32)
    yb = jnp.dot(ub_ref[...], wb_ref[...].astype(BF16), preferred_element_type=F32)
    bias = bm_ref[...]
    sa = jax.nn.sigmoid(ga_ref[...].astype(F32) + bias[0:1, :])
    sb = jax.nn.sigmoid(gb_ref[...].astype(F32) + bias[1:2, :])
    o_ref[...] = (sa * ya + sb * yb).astype(o_ref.dtype)


def _merge(ua, ub, wa, wb, gates, b_merge, *, side=None, bm=1024, bn=512):
    m, kk = ua.shape
    n = wa.shape[1]
    nb = n // bn
    grid = (m // bm, nb)
    in_specs = [
        pl.BlockSpec((bm, kk), lambda i, j: (i, 0)),
        pl.BlockSpec((bm, kk), lambda i, j: (i, 0)),
        pl.BlockSpec((kk, bn), lambda i, j: (0, j)),
        pl.BlockSpec((kk, bn), lambda i, j: (0, j)),
        pl.BlockSpec((bm, bn), lambda i, j: (i, j)),
        pl.BlockSpec((bm, bn), lambda i, j: (i, nb + j)),
        pl.BlockSpec((2, bn), lambda i, j: (0, j)),
    ]
    side = None if side is None else _SideCast(*side, grid)
    in_specs, args, out_specs, out_shape = _with_side(
        side, in_specs, [ua, ub, wa, wb, gates, gates, b_merge],
        pl.BlockSpec((bm, bn), lambda i, j: (i, j)), jax.ShapeDtypeStruct((m, n), BF16))
    return pl.pallas_call(
        functools.partial(_merge_body, has_side=side is not None),
        grid=grid,
        in_specs=in_specs,
        out_specs=out_specs,
        out_shape=out_shape,
        compiler_params=_params("arbitrary", "arbitrary"),
        name="merge",
    )(*args)


def kernel(x, ffn1_norm_g, ffn1_w_gate, ffn1_w_up, ffn1_w_down, mix_norm_g, w_in, conv_w, conv_b, w_conv_out,
           w_alpha_up, b_alpha, gla_norm_g, w_gla_out, b_merge, w_mix_out, ffn2_norm_g, ffn2_w_gate, ffn2_w_up,
           ffn2_w_down, final_norm_g):
    batch, seq, d_model = x.shape
    depth = ffn1_w_gate.shape[0]
    d_conv = conv_w.shape[-1]
    rank, d_gla_k = w_alpha_up.shape[-2:]
    d_gla_v = w_gla_out.shape[-2]
    hk, hv = d_gla_k // GLA_HEADS, d_gla_v // GLA_HEADS
    q_off = 3 * d_conv
    k_off = q_off + d_gla_k
    v_off = k_off + d_gla_k
    r_off = v_off + d_gla_v
    a_off = r_off + d_gla_v
    g_off = a_off + rank
    m = batch * seq

    xs = x.reshape(m, d_model)
    for l in range(depth):
        xs, h = _ffn(xs, ffn1_norm_g[l][None], ffn1_w_gate[l].astype(BF16), ffn1_w_up[l].astype(BF16),
                     ffn1_w_down[l].astype(BF16), mix_norm_g[l][None], "x+norm")

        wt = jnp.transpose(w_in[l])
        proj = _matmul_wt(h, wt, 0, a_off, BF16, name="in_proj_main")
        gates, w2_up = _matmul_wt(h, wt, g_off, 2 * d_model, BF16, bm=1024, bn=512, name="in_proj_gates",
                                  side=(ffn2_w_up[l], (d_model, SIDE_CAST_BLOCK)))
        la = _loggate(h, wt, a_off, w_alpha_up[l], b_alpha[l][None])

        proj3 = proj.reshape(batch, seq, a_off)
        ua = _conv_branch(proj3, conv_w[l], conv_b[l][None], d_conv)
        ub = _gla(proj3, la.reshape(batch, seq, d_gla_k), gla_norm_g[l][None],
                  q_off=q_off, k_off=k_off, v_off=v_off, r_off=r_off, hk=hk, hv=hv)
        merged, w2_down = _merge(ua.reshape(m, d_conv), ub.reshape(m, d_gla_v), w_conv_out[l], w_gla_out[l], gates,
                                 b_merge[l], side=(ffn2_w_down[l], (SIDE_CAST_BLOCK, d_model)))
        xs, w2_gate = _matmul(merged, w_mix_out[l], F32, res=xs, name="mix_out",
                              side=(ffn2_w_gate[l], (d_model, SIDE_CAST_BLOCK)))

        emit = "norm" if l == depth - 1 else "x"
        xs = _ffn(xs, ffn2_norm_g[l][None], w2_gate, w2_up, w2_down, final_norm_g[None], emit)
    return xs.reshape(batch, seq, d_model)
```

```python
import functools

import jax
import jax.numpy as jnp
import numpy as np
from jax import lax
from jax.experimental import pallas as pl
from jax.experimental.pallas import tpu as pltpu

F32 = jnp.float32
BF16 = jnp.bfloat16

EPS = 1e-6
FFN_RES = 0.5
CHUNK = 64
CONV_W = 3
GLA_HEADS = 4
GATE_TEMP = 16.0
N_LEVELS = 6
FFN_DOWN_COLS = 512
NORM_ROWS = 16
NORM_UNROLL = 4
SIDE_CAST_BLOCK = 256

V7X_VMEM_LIMIT_BYTES = 58 * 1024 * 1024
FFN_VMEM_LIMIT_BYTES = 62 * 1024 * 1024
LANES = 128
SUBLANES = 8


def _params(*sem):
    return pltpu.CompilerParams(dimension_semantics=sem, vmem_limit_bytes=V7X_VMEM_LIMIT_BYTES)


def _rms(x, g):
    ms = jnp.mean(x * x, axis=-1, keepdims=True)
    return x * lax.rsqrt(ms + EPS) * g


def _split_bf16(x):
    hi = x.astype(BF16)
    lo = (x - hi.astype(F32)).astype(BF16)
    return hi, lo


def _ffn_body(x_hbm, g_ref, wg_ref, wu_ref, wd_ref, gn_ref, *refs, emit, bm):
    if emit == "x+norm":
        o_hbm, hn_hbm, acc_ref, h_ref, rs_ref, sem_x, sem_o, sem_h = refs
    else:
        o_hbm, acc_ref, h_ref, rs_ref, sem_x, sem_o = refs
    i, f = pl.program_id(0), pl.program_id(1)
    n_i, n_f = pl.num_programs(0), pl.num_programs(1)
    slot = lax.rem(i, 2)
    acc = acc_ref.at[slot]
    d = h_ref.shape[1]
    n_slabs = bm // NORM_ROWS

    def tile(t):
        return pl.ds(pl.multiple_of(t * bm, bm), bm)

    def x_copy(t, s):
        return pltpu.make_async_copy(x_hbm.at[tile(t)], acc_ref.at[s], sem_x.at[s])

    def y_copy(t, s):
        return pltpu.make_async_copy(acc_ref.at[s], o_hbm.at[tile(t)], sem_o.at[s])

    def norm_copy(t):
        return pltpu.make_async_copy(h_ref, hn_hbm.at[tile(t)], sem_h.at[0])

    def rows(s):
        return pl.ds(pl.multiple_of(s * NORM_ROWS, NORM_ROWS), NORM_ROWS)

    @pl.when(f == 0)
    def _():
        @pl.when(i == 0)
        def _():
            x_copy(0, 0).start()

        x_copy(i, slot).wait()
        if emit == "x+norm":
            @pl.when(i > 0)
            def _():
                norm_copy(i - 1).wait()

        g = g_ref[...]

        def slab(s, carry):
            h_ref[rows(s), :] = _rms(acc[rows(s), :], g).astype(BF16)
            return carry

        lax.fori_loop(0, n_slabs, slab, 0, unroll=NORM_UNROLL)

    @pl.when(f == 1)
    def _():
        @pl.when(i > 0)
        def _():
            y_copy(i - 1, 1 - slot).wait()

        @pl.when(i + 1 < n_i)
        def _():
            x_copy(i + 1, 1 - slot).start()

    h = h_ref[...]
    gate = jnp.dot(h, wg_ref[...], preferred_element_type=F32)
    up = jnp.dot(h, wu_ref[...], preferred_element_type=F32)
    act = (FFN_RES * (gate * jax.nn.sigmoid(gate) * up)).astype(BF16)
    for n0 in range(0, d, FFN_DOWN_COLS):
        sl = slice(n0, n0 + FFN_DOWN_COLS)
        acc[:, sl] += jnp.dot(act, wd_ref[:, sl], preferred_element_type=F32)

    @pl.when(f == n_f - 1)
    def _():
        if emit != "x":
            gn = gn_ref[...]

            def stat_slab(s, carry):
                y = acc[rows(s), :]
                rs_ref[rows(s), :] = lax.rsqrt(jnp.mean(y * y, axis=-1, keepdims=True) + EPS)
                return carry

            lax.fori_loop(0, n_slabs, stat_slab, 0, unroll=NORM_UNROLL)

            def norm_slab(s, carry):
                normed = acc[rows(s), :] * rs_ref[rows(s), :] * gn
                if emit == "norm":
                    acc[rows(s), :] = normed
                else:
                    h_ref[rows(s), :] = normed.astype(BF16)
                return carry

            lax.fori_loop(0, n_slabs, norm_slab, 0, unroll=NORM_UNROLL)

        y_copy(i, slot).start()
        if emit == "x+norm":
            norm_copy(i).start()

        @pl.when(i == n_i - 1)
        def _():
            y_copy(i, slot).wait()
            if emit == "x+norm":
                norm_copy(i).wait()


def _ffn(x, g, wg, wu, wd, g_next, emit, *, bm=1024, bf=256):
    m, d = x.shape
    dff = wg.shape[1]
    assert m % bm == 0 and dff // bf >= 2
    vec = pl.BlockSpec((1, d), lambda i, f: (0, 0))
    hbm = pl.BlockSpec(memory_space=pl.ANY)
    out_specs, out_shape = hbm, jax.ShapeDtypeStruct((m, d), F32)
    scratch = [
        pltpu.VMEM((2, bm, d), F32),
        pltpu.VMEM((bm, d), BF16),
        pltpu.VMEM((bm, 1), F32),
        pltpu.SemaphoreType.DMA((2,)),
        pltpu.SemaphoreType.DMA((2,)),
    ]
    if emit == "x+norm":
        out_specs, out_shape = [hbm, hbm], [out_shape, jax.ShapeDtypeStruct((m, d), BF16)]
        scratch.append(pltpu.SemaphoreType.DMA((1,)))
    return pl.pallas_call(
        functools.partial(_ffn_body, emit=emit, bm=bm),
        grid=(m // bm, dff // bf),
        in_specs=[
            hbm,
            vec,
            pl.BlockSpec((d, bf), lambda i, f: (0, f)),
            pl.BlockSpec((d, bf), lambda i, f: (0, f)),
            pl.BlockSpec((bf, d), lambda i, f: (f, 0)),
            vec,
        ],
        out_specs=out_specs,
        out_shape=out_shape,
        scratch_shapes=scratch,
        compiler_params=pltpu.CompilerParams(dimension_semantics=("arbitrary", "arbitrary"),
                                             vmem_limit_bytes=FFN_VMEM_LIMIT_BYTES),
        name="ffn",
    )(x, g, wg, wu, wd, g_next)


class _SideCast:
    def __init__(self, src, block, grid):
        rows, cols = src.shape
        br, bc = block
        assert rows % br == 0 and cols % bc == 0 and (br == rows or bc == cols)
        n_blocks = (rows // br) * (cols // bc)
        assert grid[0] * grid[1] >= n_blocks
        n_inner = grid[1]

        def index(i, j):
            t = jnp.minimum(i * n_inner + j, n_blocks - 1)
            return (t, 0) if bc == cols else (0, t)

        self.src = src
        self.in_spec = pl.BlockSpec(block, index)
        self.out_spec = pl.BlockSpec(block, index)
        self.out_shape = jax.ShapeDtypeStruct(src.shape, BF16)

    @staticmethod
    def step(src_ref, dst_ref):
        dst_ref[...] = src_ref[...].astype(BF16)


def _with_side(side, in_specs, args, out_specs, out_shape):
    if side is None:
        return in_specs, args, out_specs, out_shape
    return in_specs + [side.in_spec], args + [side.src], [out_specs, side.out_spec], [out_shape, side.out_shape]


def _mm_body(*refs, has_res, has_side):
    a_ref, w_ref = refs[:2]
    refs = refs[2:]
    r_ref = None
    if has_res:
        r_ref, refs = refs[0], refs[1:]
    if has_side:
        side_src, o_ref, side_dst = refs
        _SideCast.step(side_src, side_dst)
    else:
        (o_ref,) = refs
    acc = jnp.dot(a_ref[...], w_ref[...].astype(BF16), preferred_element_type=F32)
    if has_res:
        acc = r_ref[...] + acc
    o_ref[...] = acc.astype(o_ref.dtype)


def _matmul(a, w, out_dtype, *, res=None, side=None, bm=1024, bn=512, name="matmul"):
    m, k = a.shape
    n = w.shape[1]
    bn = min(bn, n)
    grid = (m // bm, n // bn)
    in_specs = [pl.BlockSpec((bm, k), lambda i, j: (i, 0)), pl.BlockSpec((k, bn), lambda i, j: (0, j))]
    args = [a, w]
    if res is not None:
        in_specs.append(pl.BlockSpec((bm, bn), lambda i, j: (i, j)))
        args.append(res)
    side = None if side is None else _SideCast(*side, grid)
    in_specs, args, out_specs, out_shape = _with_side(
        side, in_specs, args, pl.BlockSpec((bm, bn), lambda i, j: (i, j)), jax.ShapeDtypeStruct((m, n), out_dtype))
    return pl.pallas_call(
        functools.partial(_mm_body, has_res=res is not None, has_side=side is not None),
        grid=grid,
        in_specs=in_specs,
        out_specs=out_specs,
        out_shape=out_shape,
        compiler_params=_params("arbitrary", "arbitrary"),
        name=name,
    )(*args)


def _mm_wt_body(a_ref, wt_ref, *refs, has_side):
    if has_side:
        side_src, o_ref, side_dst, wbf_ref = refs
        _SideCast.step(side_src, side_dst)
    else:
        o_ref, wbf_ref = refs

    @pl.when(pl.program_id(1) == 0)
    def _():
        wbf_ref[...] = wt_ref[...].T.astype(BF16)

    o_ref[...] = jnp.dot(a_ref[...], wbf_ref[...], preferred_element_type=F32).astype(o_ref.dtype)


def _matmul_wt(a, wt, row_off, n, out_dtype, *, side=None, bm=512, bn=1024, name="matmul_wt"):
    m, k = a.shape
    assert row_off % SUBLANES == 0 and bn % SUBLANES == 0
    grid = (n // bn, m // bm)
    in_specs = [
        pl.BlockSpec((bm, k), lambda j, i: (i, 0)),
        pl.BlockSpec((pl.Element(bn), pl.Element(k)), lambda j, i: (pl.multiple_of(row_off + j * bn, SUBLANES), 0)),
    ]
    side = None if side is None else _SideCast(*side, grid)
    in_specs, args, out_specs, out_shape = _with_side(
        side, in_specs, [a, wt], pl.BlockSpec((bm, bn), lambda j, i: (i, j)), jax.ShapeDtypeStruct((m, n), out_dtype))
    return pl.pallas_call(
        functools.partial(_mm_wt_body, has_side=side is not None),
        grid=grid,
        in_specs=in_specs,
        out_specs=out_specs,
        out_shape=out_shape,
        scratch_shapes=[pltpu.VMEM((k, bn), BF16)],
        compiler_params=_params("arbitrary", "arbitrary"),
        name=name,
    )(*args)


def _conv_body(cb_ref, cc_ref, cu_ref, w_ref, b_ref, o_ref):
    p = cc_ref[0].astype(F32) * cu_ref[0].astype(F32)
    row = lax.broadcasted_iota(jnp.int32, p.shape, 0)
    p1 = jnp.where(row >= 1, pltpu.roll(p, 1, axis=0), 0.0)
    p2 = jnp.where(row >= 2, pltpu.roll(p, 2, axis=0), 0.0)
    w = w_ref[...]
    conv = b_ref[...] + w[0:1, :] * p2
    conv = conv + w[1:2, :] * p1
    conv = conv + w[2:3, :] * p
    o_ref[0] = (cb_ref[0].astype(F32) * conv).astype(o_ref.dtype)


def _conv_branch(proj3, conv_w, conv_b, d_conv, *, bc=256):
    b, s, _ = proj3.shape
    nb = d_conv // bc
    return pl.pallas_call(
        _conv_body,
        grid=(b, nb),
        in_specs=[
            pl.BlockSpec((1, s, bc), lambda i, j: (i, 0, j)),
            pl.BlockSpec((1, s, bc), lambda i, j: (i, 0, nb + j)),
            pl.BlockSpec((1, s, bc), lambda i, j: (i, 0, 2 * nb + j)),
            pl.BlockSpec((CONV_W, bc), lambda i, j: (0, j)),
            pl.BlockSpec((1, bc), lambda i, j: (0, j)),
        ],
        out_specs=pl.BlockSpec((1, s, bc), lambda i, j: (i, 0, j)),
        out_shape=jax.ShapeDtypeStruct((b, s, d_conv), BF16),
        compiler_params=_params("parallel", "parallel"),
        name="conv",
    )(proj3, proj3, proj3, conv_w, conv_b)


def _loggate_body(h_ref, wl_ref, wup_ref, ba_ref, o_ref):
    nt = (((1,), (1,)), ((), ()))
    a = lax.dot_general(h_ref[...], wl_ref[...].astype(BF16), nt, preferred_element_type=F32)
    a_hi, a_lo = _split_bf16(a)
    w_hi, w_lo = _split_bf16(wup_ref[...])
    z = (jnp.dot(a_hi, w_hi, preferred_element_type=F32) + jnp.dot(a_hi, w_lo, preferred_element_type=F32)
         + jnp.dot(a_lo, w_hi, preferred_element_type=F32)) + ba_ref[...]
    o_ref[...] = (jnp.minimum(z, 0.0) - jnp.log1p(jnp.exp(-jnp.abs(z)))) / GATE_TEMP


def _loggate(h, wt, row_off, w_up, b_alpha, *, bm=1024):
    m, d = h.shape
    rank, dk = w_up.shape
    return pl.pallas_call(
        _loggate_body,
        grid=(m // bm,),
        in_specs=[
            pl.BlockSpec((bm, d), lambda i: (i, 0)),
            pl.BlockSpec((pl.Element(rank), pl.Element(d)), lambda i: (row_off, 0)),
            pl.BlockSpec((rank, dk), lambda i: (0, 0)),
            pl.BlockSpec((1, dk), lambda i: (0, 0)),
        ],
        out_specs=pl.BlockSpec((bm, dk), lambda i: (i, 0)),
        out_shape=jax.ShapeDtypeStruct((m, dk), F32),
        compiler_params=_params("parallel"),
        name="loggate",
    )(h, wt, w_up, b_alpha)


def _gla_tables():
    c = CHUNK
    i = np.arange(c)[:, None]
    t = np.arange(c)[None, :]
    blocks = [(t <= i), (t > i)]
    masks = []
    for lvl in range(N_LEVELS):
        p = c >> lvl
        half = p // 2
        mid = (i // p) * p + half
        upper = i >= mid
        blocks.append(np.where(upper, (t > mid) & (t <= i), (t > i) & (t <= mid)))
        j = t
        same_parent = (i // p) == (j // p)
        masks.append(same_parent & (((i % p) >= half) != ((j % p) >= half)))
    expo = np.concatenate(blocks, axis=0).astype(np.float32)
    mask = np.stack(masks).astype(np.float32)
    return expo, mask


def _gla_body(q_ref, k_ref, v_ref, r_ref, la_ref, gn_ref, expo_ref, mask_ref, o_ref, state_ref, *, scale, hk, hv):
    c = CHUNK
    nt = (((1,), (1,)), ((), ()))
    tn = (((0,), (0,)), ((), ()))

    @pl.when(pl.program_id(1) == 0)
    def _():
        state_ref[...] = jnp.zeros_like(state_ref)

    row = lax.broadcasted_iota(jnp.int32, (c, c), 0)
    col = lax.broadcasted_iota(jnp.int32, (c, c), 1)
    eye = row == col
    for h in range(GLA_HEADS):
        ks = slice(h * hk, (h + 1) * hk)
        vs = slice(h * hv, (h + 1) * hv)
        q = q_ref[0, :, ks].astype(F32) * scale
        k = k_ref[0, :, ks].astype(F32)
        v = v_ref[0, :, vs]

        hi, lo = _split_bf16(la_ref[0, :, ks])
        expo = jnp.dot(expo_ref[...], jnp.concatenate([hi, lo], axis=0), preferred_element_type=F32)
        e_all = jnp.exp(expo)
        e_l = e_all[0:c]
        e_rest = e_all[c:2 * c]

        scores = jnp.where(eye, jnp.sum(q * k, axis=-1, keepdims=True), 0.0)
        for lvl in range(N_LEVELS):
            e = e_all[(2 + lvl) * c:(3 + lvl) * c]
            p = lax.dot_general((q * e).astype(BF16), (k * e).astype(BF16), nt, preferred_element_type=F32)
            scores = scores + mask_ref[lvl] * p

        state = state_ref[h]
        o = jnp.dot(scores.astype(BF16), v, preferred_element_type=F32)
        o = o + lax.dot_general((q * e_l).astype(BF16), state.astype(BF16), nt, preferred_element_type=F32)

        k_dec = (k * e_rest).astype(BF16)
        state_ref[h] = state * e_l[c - 1:c, :] + lax.dot_general(v, k_dec, tn, preferred_element_type=F32)

        r = r_ref[0, :, vs].astype(F32)
        o_ref[0, :, vs] = (_rms(o, gn_ref[...]) * (r * jax.nn.sigmoid(r))).astype(o_ref.dtype)


def _gla(proj3, la3, gn, *, q_off, k_off, v_off, r_off, hk, hv):
    b, s, _ = proj3.shape
    nc = s // CHUNK
    dk, dv = GLA_HEADS * hk, GLA_HEADS * hv
    expo, mask = _gla_tables()
    expo2 = np.concatenate([expo, expo], axis=1)
    body = functools.partial(_gla_body, scale=float(hk) ** -0.5, hk=hk, hv=hv)
    return pl.pallas_call(
        body,
        grid=(b, nc),
        in_specs=[
            pl.BlockSpec((1, CHUNK, dk), lambda i, c: (i, c, q_off // dk)),
            pl.BlockSpec((1, CHUNK, dk), lambda i, c: (i, c, k_off // dk)),
            pl.BlockSpec((1, CHUNK, dv), lambda i, c: (i, c, v_off // dv)),
            pl.BlockSpec((1, CHUNK, dv), lambda i, c: (i, c, r_off // dv)),
            pl.BlockSpec((1, CHUNK, dk), lambda i, c: (i, c, 0)),
            pl.BlockSpec((1, hv), lambda i, c: (0, 0)),
            pl.BlockSpec(expo2.shape, lambda i, c: (0, 0)),
            pl.BlockSpec(mask.shape, lambda i, c: (0, 0, 0)),
        ],
        out_specs=pl.BlockSpec((1, CHUNK, dv), lambda i, c: (i, c, 0)),
        out_shape=jax.ShapeDtypeStruct((b, s, dv), BF16),
        scratch_shapes=[pltpu.VMEM((GLA_HEADS, hv, hk), F32)],
        compiler_params=_params("parallel", "arbitrary"),
        name="gla",
    )(proj3, proj3, proj3, proj3, la3, gn, jnp.asarray(expo2, dtype=BF16), jnp.asarray(mask))


def _merge_body(ua_ref, ub_ref, wa_ref, wb_ref, ga_ref, gb_ref, bm_ref, *refs, has_side):
    if has_side:
        side_src, o_ref, side_dst = refs
        _SideCast.step(side_src, side_dst)
    else:
        (o_ref,) = refs
    ya = jnp.dot(ua_ref[...], wa_ref[...].astype(BF16), preferred_element_type=F32)
    yb = jnp.dot(ub_ref[...], wb_ref[...].astype(BF16), preferred_element_type=F32)
    bias = bm_ref[...]
    sa = jax.nn.sigmoid(ga_ref[...].astype(F32) + bias[0:1, :])
    sb = jax.nn.sigmoid(gb_ref[...].astype(F32) + bias[1:2, :])
    o_ref[...] = (sa * ya + sb * yb).astype(o_ref.dtype)


def _merge(ua, ub, wa, wb, gates, b_merge, *, side=None, bm=1024, bn=512):
    m, kk = ua.shape
    n = wa.shape[1]
    nb = n // bn
    grid = (m // bm, nb)
    in_specs = [
        pl.BlockSpec((bm, kk), lambda i, j: (i, 0)),
        pl.BlockSpec((bm, kk), lambda i, j: (i, 0)),
        pl.BlockSpec((kk, bn), lambda i, j: (0, j)),
        pl.BlockSpec((kk, bn), lambda i, j: (0, j)),
        pl.BlockSpec((bm, bn), lambda i, j: (i, j)),
        pl.BlockSpec((bm, bn), lambda i, j: (i, nb + j)),
        pl.BlockSpec((2, bn), lambda i, j: (0, j)),
    ]
    side = None if side is None else _SideCast(*side, grid)
    in_specs, args, out_specs, out_shape = _with_side(
        side, in_specs, [ua, ub, wa, wb, gates, gates, b_merge],
        pl.BlockSpec((bm, bn), lambda i, j: (i, j)), jax.ShapeDtypeStruct((m, n), BF16))
    return pl.pallas_call(
        functools.partial(_merge_body, has_side=side is not None),
        grid=grid,
        in_specs=in_specs,
        out_specs=out_specs,
        out_shape=out_shape,
        compiler_params=_params("arbitrary", "arbitrary"),
        name="merge",
    )(*args)


def kernel(x, ffn1_norm_g, ffn1_w_gate, ffn1_w_up, ffn1_w_down, mix_norm_g, w_in, conv_w, conv_b, w_conv_out,
           w_alpha_up, b_alpha, gla_norm_g, w_gla_out, b_merge, w_mix_out, ffn2_norm_g, ffn2_w_gate, ffn2_w_up,
           ffn2_w_down, final_norm_g):
    batch, seq, d_model = x.shape
    depth = ffn1_w_gate.shape[0]
    d_conv = conv_w.shape[-1]
    rank, d_gla_k = w_alpha_up.shape[-2:]
    d_gla_v = w_gla_out.shape[-2]
    hk, hv = d_gla_k // GLA_HEADS, d_gla_v // GLA_HEADS
    q_off = 3 * d_conv
    k_off = q_off + d_gla_k
    v_off = k_off + d_gla_k
    r_off = v_off + d_gla_v
    a_off = r_off + d_gla_v
    g_off = a_off + rank
    m = batch * seq

    xs = x.reshape(m, d_model)
    for l in range(depth):
        xs, h = _ffn(xs, ffn1_norm_g[l][None], ffn1_w_gate[l].astype(BF16), ffn1_w_up[l].astype(BF16),
                     ffn1_w_down[l].astype(BF16), mix_norm_g[l][None], "x+norm")

        wt = jnp.transpose(w_in[l])
        proj = _matmul_wt(h, wt, 0, a_off, BF16, name="in_proj_main")
        gates, w2_up = _matmul_wt(h, wt, g_off, 2 * d_model, BF16, bm=1024, bn=512, name="in_proj_gates",
                                  side=(ffn2_w_up[l], (d_model, SIDE_CAST_BLOCK)))
        la = _loggate(h, wt, a_off, w_alpha_up[l], b_alpha[l][None])

        proj3 = proj.reshape(batch, seq, a_off)
        ua = _conv_branch(proj3, conv_w[l], conv_b[l][None], d_conv)
        ub = _gla(proj3, la.reshape(batch, seq, d_gla_k), gla_norm_g[l][None],
                  q_off=q_off, k_off=k_off, v_off=v_off, r_off=r_off, hk=hk, hv=hv)
        merged, w2_down = _merge(ua.reshape(m, d_conv), ub.reshape(m, d_gla_v), w_conv_out[l], w_gla_out[l], gates,
                                 b_merge[l], side=(ffn2_w_down[l], (SIDE_CAST_BLOCK, d_model)))
        xs, w2_gate = _matmul(merged, w_mix_out[l], F32, res=xs, name="mix_out",
                              side=(ffn2_w_gate[l], (d_model, SIDE_CAST_BLOCK)))

        emit = "norm" if l == depth - 1 else "x"
        xs = _ffn(xs, ffn2_norm_g[l][None], w2_gate, w2_up, w2_down, final_norm_g[None], emit)
    return xs.reshape(batch, seq, d_model)
```

```python
import functools

import jax
import jax.numpy as jnp
import numpy as np
from jax import lax
from jax.experimental import pallas as pl
from jax.experimental.pallas import tpu as pltpu

F32 = jnp.float32
BF16 = jnp.bfloat16

EPS = 1e-6
FFN_RES = 0.5
CHUNK = 64
CONV_W = 3
GLA_HEADS = 4
GATE_TEMP = 16.0
N_LEVELS = 6
FFN_DOWN_COLS = 512
NORM_ROWS = 16
NORM_UNROLL = 4
IN_PROJ_SIDE_COLS = 128
GLA_SIDE_ROWS = 256

V7X_VMEM_LIMIT_BYTES = 58 * 1024 * 1024
V7X_VMEM_LIMIT_LARGE_BYTES = 62 * 1024 * 1024
LANES = 128
SUBLANES = 8
V7X_MXU_COLS = 256


def _params(*sem, vmem_limit_bytes=V7X_VMEM_LIMIT_BYTES):
    return pltpu.CompilerParams(dimension_semantics=sem, vmem_limit_bytes=vmem_limit_bytes)


def _rms(x, g):
    ms = jnp.mean(x * x, axis=-1, keepdims=True)
    return x * lax.rsqrt(ms + EPS) * g


def _split_bf16(x):
    hi = x.astype(BF16)
    lo = (x - hi.astype(F32)).astype(BF16)
    return hi, lo


def _ffn_body(x_hbm, g_ref, wg_ref, wu_ref, wd_ref, gn_ref, *refs, emit, bm):
    if emit == "x+norm":
        o_hbm, hn_hbm, acc_ref, h_ref, rs_ref, sem_x, sem_o, sem_h = refs
    else:
        o_hbm, acc_ref, h_ref, rs_ref, sem_x, sem_o = refs
    i, f = pl.program_id(0), pl.program_id(1)
    n_i, n_f = pl.num_programs(0), pl.num_programs(1)
    slot = lax.rem(i, 2)
    acc = acc_ref.at[slot]
    d = h_ref.shape[1]
    n_slabs = bm // NORM_ROWS

    def tile(t):
        return pl.ds(pl.multiple_of(t * bm, bm), bm)

    def x_copy(t, s):
        return pltpu.make_async_copy(x_hbm.at[tile(t)], acc_ref.at[s], sem_x.at[s])

    def y_copy(t, s):
        return pltpu.make_async_copy(acc_ref.at[s], o_hbm.at[tile(t)], sem_o.at[s])

    def norm_copy(t):
        return pltpu.make_async_copy(h_ref, hn_hbm.at[tile(t)], sem_h.at[0])

    def rows(s):
        return pl.ds(pl.multiple_of(s * NORM_ROWS, NORM_ROWS), NORM_ROWS)

    @pl.when(f == 0)
    def _():
        @pl.when(i == 0)
        def _():
            x_copy(0, 0).start()

        x_copy(i, slot).wait()
        if emit == "x+norm":
            @pl.when(i > 0)
            def _():
                norm_copy(i - 1).wait()

        g = g_ref[...]

        def slab(s, carry):
            h_ref[rows(s), :] = _rms(acc[rows(s), :], g).astype(BF16)
            return carry

        lax.fori_loop(0, n_slabs, slab, 0, unroll=NORM_UNROLL)

    @pl.when(f == 1)
    def _():
        @pl.when(i > 0)
        def _():
            y_copy(i - 1, 1 - slot).wait()

        @pl.when(i + 1 < n_i)
        def _():
            x_copy(i + 1, 1 - slot).start()

    h = h_ref[...]
    gate = jnp.dot(h, wg_ref[...], preferred_element_type=F32)
    up = jnp.dot(h, wu_ref[...], preferred_element_type=F32)
    act = (FFN_RES * (gate * jax.nn.sigmoid(gate) * up)).astype(BF16)
    for n0 in range(0, d, FFN_DOWN_COLS):
        sl = slice(n0, n0 + FFN_DOWN_COLS)
        acc[:, sl] += jnp.dot(act, wd_ref[:, sl], preferred_element_type=F32)

    @pl.when(f == n_f - 1)
    def _():
        if emit != "x":
            gn = gn_ref[...]

            def stat_slab(s, carry):
                y = acc[rows(s), :]
                rs_ref[rows(s), :] = lax.rsqrt(jnp.mean(y * y, axis=-1, keepdims=True) + EPS)
                return carry

            lax.fori_loop(0, n_slabs, stat_slab, 0, unroll=NORM_UNROLL)

            def norm_slab(s, carry):
                normed = acc[rows(s), :] * rs_ref[rows(s), :] * gn
                if emit == "norm":
                    acc[rows(s), :] = normed
                else:
                    h_ref[rows(s), :] = normed.astype(BF16)
                return carry

            lax.fori_loop(0, n_slabs, norm_slab, 0, unroll=NORM_UNROLL)

        y_copy(i, slot).start()
        if emit == "x+norm":
            norm_copy(i).start()

        @pl.when(i == n_i - 1)
        def _():
            y_copy(i, slot).wait()
            if emit == "x+norm":
                norm_copy(i).wait()


def _ffn(x, g, wg, wu, wd, g_next, emit, *, bm=1024, bf=256):
    m, d = x.shape
    dff = wg.shape[1]
    assert m % bm == 0 and dff // bf >= 2
    vec = pl.BlockSpec((1, d), lambda i, f: (0, 0))
    hbm = pl.BlockSpec(memory_space=pl.ANY)
    out_specs, out_shape = hbm, jax.ShapeDtypeStruct((m, d), F32)
    scratch = [
        pltpu.VMEM((2, bm, d), F32),
        pltpu.VMEM((bm, d), BF16),
        pltpu.VMEM((bm, 1), F32),
        pltpu.SemaphoreType.DMA((2,)),
        pltpu.SemaphoreType.DMA((2,)),
    ]
    if emit == "x+norm":
        out_specs, out_shape = [hbm, hbm], [out_shape, jax.ShapeDtypeStruct((m, d), BF16)]
        scratch.append(pltpu.SemaphoreType.DMA((1,)))
    return pl.pallas_call(
        functools.partial(_ffn_body, emit=emit, bm=bm),
        grid=(m // bm, dff // bf),
        in_specs=[
            hbm,
            vec,
            pl.BlockSpec((d, bf), lambda i, f: (0, f)),
            pl.BlockSpec((d, bf), lambda i, f: (0, f)),
            pl.BlockSpec((bf, d), lambda i, f: (f, 0)),
            vec,
        ],
        out_specs=out_specs,
        out_shape=out_shape,
        scratch_shapes=scratch,
        compiler_params=_params("arbitrary", "arbitrary", vmem_limit_bytes=V7X_VMEM_LIMIT_LARGE_BYTES),
        name="ffn",
    )(x, g, wg, wu, wd, g_next)


class _SideCast:
    def __init__(self, src, block, grid):
        rows, cols = src.shape
        br, bc = block
        assert rows % br == 0 and cols % bc == 0 and (br == rows or bc == cols)
        n_blocks = (rows // br) * (cols // bc)
        assert grid[0] * grid[1] >= n_blocks
        n_inner = grid[1]

        def index(i, j):
            t = jnp.minimum(i * n_inner + j, n_blocks - 1)
            return (t, 0) if bc == cols else (0, t)

        self.src = src
        self.in_spec = pl.BlockSpec(block, index)
        self.out_spec = pl.BlockSpec(block, index)
        self.out_shape = jax.ShapeDtypeStruct(src.shape, BF16)

    @staticmethod
    def step(src_ref, dst_ref):
        dst_ref[...] = src_ref[...].astype(BF16)


def _mm_res_body(a_ref, w_ref, r_ref, o_ref):
    o_ref[...] = r_ref[...] + jnp.dot(a_ref[...], w_ref[...], preferred_element_type=F32)


def _matmul_res(a, w, res, *, bm=1024, bn=512):
    m, k = a.shape
    n = w.shape[1]
    return pl.pallas_call(
        _mm_res_body,
        grid=(m // bm, n // bn),
        in_specs=[
            pl.BlockSpec((bm, k), lambda i, j: (i, 0)),
            pl.BlockSpec((k, bn), lambda i, j: (0, j)),
            pl.BlockSpec((bm, bn), lambda i, j: (i, j)),
        ],
        out_specs=pl.BlockSpec((bm, bn), lambda i, j: (i, j)),
        out_shape=jax.ShapeDtypeStruct((m, n), F32),
        compiler_params=_params("parallel", "arbitrary"),
        name="mix_out",
    )(a, w, res)


def _in_proj_body(a_ref, piece_ref, *refs, n_side):
    side_srcs, o_ref, side_dsts, wbf_ref = refs[:n_side], refs[n_side], refs[n_side + 1:-1], refs[-1]
    jj, i = pl.program_id(0), pl.program_id(1)
    for side_src, side_dst in zip(side_srcs, side_dsts):
        _SideCast.step(side_src, side_dst)

    def stage():
        wbf_ref[lax.rem(jj, 2), i] = piece_ref[...].T.astype(BF16)

    @pl.when(jj == 0)
    def _():
        stage()

    @pl.when(jj > 0)
    def _():
        stage()
        w = wbf_ref.at[lax.rem(jj + 1, 2)]
        a = a_ref[...]
        piece = w.shape[2]
        group = V7X_MXU_COLS // piece
        for p in range(0, w.shape[0], group):
            wp = jnp.concatenate([w[p + q] for q in range(group)], axis=1)
            o_ref[:, p * piece:(p + group) * piece] = jnp.dot(a, wp, preferred_element_type=F32).astype(o_ref.dtype)


def _in_proj(a, wt, groups, sides, *, bm=1024, tile=1024):
    m, k = a.shape
    mi = m // bm
    piece = tile // mi
    assert piece % LANES == 0 and V7X_MXU_COLS % piece == 0
    (off0, rows0), (off1, rows1) = groups
    assert off0 % SUBLANES == 0 and off1 % SUBLANES == 0 and rows0 % tile == 0 and rows1 % tile == 0
    n0 = rows0 // tile
    n_tiles = n0 + rows1 // tile
    grid = (n_tiles + 1, mi)

    def piece_index(jj, i):
        t = jnp.minimum(jj, n_tiles - 1)
        p = jnp.where(jj < n_tiles, i, mi - 1)
        row = jnp.where(t < n0, off0 + t * tile, off1 + (t - n0) * tile) + p * piece
        return pl.multiple_of(row, SUBLANES), 0

    in_specs = [
        pl.BlockSpec((bm, k), lambda jj, i: (jnp.where(jj > 0, i, 0), 0)),
        pl.BlockSpec((pl.Element(piece), pl.Element(k)), piece_index),
    ]
    out_specs = [pl.BlockSpec((bm, tile), lambda jj, i: (jnp.where(jj > 0, i, 0), jnp.maximum(jj - 1, 0)))]
    out_shape = [jax.ShapeDtypeStruct((m, n_tiles * tile), BF16)]
    args = [a, wt]
    for src, block in sides:
        side = _SideCast(src, block, grid)
        in_specs.append(side.in_spec)
        args.append(side.src)
        out_specs.append(side.out_spec)
        out_shape.append(side.out_shape)
    return pl.pallas_call(
        functools.partial(_in_proj_body, n_side=len(sides)),
        grid=grid,
        in_specs=in_specs,
        out_specs=out_specs,
        out_shape=out_shape,
        scratch_shapes=[pltpu.VMEM((2, mi, k, piece), BF16)],
        compiler_params=_params("arbitrary", "arbitrary", vmem_limit_bytes=V7X_VMEM_LIMIT_LARGE_BYTES),
        name="in_proj",
    )(*args)


def _conv_body(cb_ref, cc_ref, cu_ref, w_ref, b_ref, o_ref):
    p = cc_ref[0].astype(F32) * cu_ref[0].astype(F32)
    row = lax.broadcasted_iota(jnp.int32, p.shape, 0)
    p1 = jnp.where(row >= 1, pltpu.roll(p, 1, axis=0), 0.0)
    p2 = jnp.where(row >= 2, pltpu.roll(p, 2, axis=0), 0.0)
    w = w_ref[...]
    conv = b_ref[...] + w[0:1, :] * p2
    conv = conv + w[1:2, :] * p1
    conv = conv + w[2:3, :] * p
    o_ref[0] = (cb_ref[0].astype(F32) * conv).astype(o_ref.dtype)


def _conv_branch(proj3, conv_w, conv_b, d_conv, *, bc=256):
    b, s, _ = proj3.shape
    nb = d_conv // bc
    return pl.pallas_call(
        _conv_body,
        grid=(b, nb),
        in_specs=[
            pl.BlockSpec((1, s, bc), lambda i, j: (i, 0, j)),
            pl.BlockSpec((1, s, bc), lambda i, j: (i, 0, nb + j)),
            pl.BlockSpec((1, s, bc), lambda i, j: (i, 0, 2 * nb + j)),
            pl.BlockSpec((CONV_W, bc), lambda i, j: (0, j)),
            pl.BlockSpec((1, bc), lambda i, j: (0, j)),
        ],
        out_specs=pl.BlockSpec((1, s, bc), lambda i, j: (i, 0, j)),
        out_shape=jax.ShapeDtypeStruct((b, s, d_conv), BF16),
        compiler_params=_params("parallel", "parallel"),
        name="conv",
    )(proj3, proj3, proj3, conv_w, conv_b)


def _loggate_body(h_ref, wl_ref, wup_ref, ba_ref, o_ref):
    nt = (((1,), (1,)), ((), ()))
    a = lax.dot_general(h_ref[...], wl_ref[...].astype(BF16), nt, preferred_element_type=F32)
    a_hi, a_lo = _split_bf16(a)
    w_hi, w_lo = _split_bf16(wup_ref[...])
    z = (jnp.dot(a_hi, w_hi, preferred_element_type=F32) + jnp.dot(a_hi, w_lo, preferred_element_type=F32)
         + jnp.dot(a_lo, w_hi, preferred_element_type=F32)) + ba_ref[...]
    o_ref[...] = (jnp.minimum(z, 0.0) - jnp.log1p(jnp.exp(-jnp.abs(z)))) / GATE_TEMP


def _loggate(h, wt, row_off, w_up, b_alpha, *, bm=1024):
    m, d = h.shape
    rank, dk = w_up.shape
    return pl.pallas_call(
        _loggate_body,
        grid=(m // bm,),
        in_specs=[
            pl.BlockSpec((bm, d), lambda i: (i, 0)),
            pl.BlockSpec((pl.Element(rank), pl.Element(d)), lambda i: (row_off, 0)),
            pl.BlockSpec((rank, dk), lambda i: (0, 0)),
            pl.BlockSpec((1, dk), lambda i: (0, 0)),
        ],
        out_specs=pl.BlockSpec((bm, dk), lambda i: (i, 0)),
        out_shape=jax.ShapeDtypeStruct((m, dk), F32),
        compiler_params=_params("parallel"),
        name="loggate",
    )(h, wt, w_up, b_alpha)


def _gla_tables():
    c = CHUNK
    i = np.arange(c)[:, None]
    t = np.arange(c)[None, :]
    blocks = [(t <= i), (t > i)]
    masks = []
    for lvl in range(N_LEVELS):
        p = c >> lvl
        half = p // 2
        mid = (i // p) * p + half
        upper = i >= mid
        blocks.append(np.where(upper, (t > mid) & (t <= i), (t > i) & (t <= mid)))
        j = t
        same_parent = (i // p) == (j // p)
        masks.append(same_parent & (((i % p) >= half) != ((j % p) >= half)))
    expo = np.concatenate(blocks, axis=0).astype(np.float32)
    mask = np.stack(masks).astype(np.float32)
    return expo, mask


def _gla_body(q_ref, k_ref, v_ref, r_ref, la_ref, gn_ref, expo_ref, mask_ref, side_src, o_ref, side_dst, state_ref, *,
              scale, hk, hv):
    c = CHUNK
    _SideCast.step(side_src, side_dst)
    nt = (((1,), (1,)), ((), ()))
    tn = (((0,), (0,)), ((), ()))

    @pl.when(pl.program_id(1) == 0)
    def _():
        state_ref[...] = jnp.zeros_like(state_ref)

    row = lax.broadcasted_iota(jnp.int32, (c, c), 0)
    col = lax.broadcasted_iota(jnp.int32, (c, c), 1)
    eye = row == col
    for h in range(GLA_HEADS):
        ks = slice(h * hk, (h + 1) * hk)
        vs = slice(h * hv, (h + 1) * hv)
        q = q_ref[0, :, ks].astype(F32) * scale
        k = k_ref[0, :, ks].astype(F32)
        v = v_ref[0, :, vs]

        hi, lo = _split_bf16(la_ref[0, :, ks])
        expo = jnp.dot(expo_ref[...], jnp.concatenate([hi, lo], axis=0), preferred_element_type=F32)
        e_all = jnp.exp(expo)
        e_l = e_all[0:c]
        e_rest = e_all[c:2 * c]

        scores = jnp.where(eye, jnp.sum(q * k, axis=-1, keepdims=True), 0.0)
        for lvl in range(N_LEVELS):
            e = e_all[(2 + lvl) * c:(3 + lvl) * c]
            p = lax.dot_general((q * e).astype(BF16), (k * e).astype(BF16), nt, preferred_element_type=F32)
            scores = scores + mask_ref[lvl] * p

        state = state_ref[h]
        o = jnp.dot(scores.astype(BF16), v, preferred_element_type=F32)
        o = o + lax.dot_general((q * e_l).astype(BF16), state.astype(BF16), nt, preferred_element_type=F32)

        k_dec = (k * e_rest).astype(BF16)
        state_ref[h] = state * e_l[c - 1:c, :] + lax.dot_general(v, k_dec, tn, preferred_element_type=F32)

        r = r_ref[0, :, vs].astype(F32)
        o_ref[0, :, vs] = (_rms(o, gn_ref[...]) * (r * jax.nn.sigmoid(r))).astype(o_ref.dtype)


def _gla(proj3, la3, gn, side, *, q_off, k_off, v_off, r_off, hk, hv):
    b, s, _ = proj3.shape
    nc = s // CHUNK
    dk, dv = GLA_HEADS * hk, GLA_HEADS * hv
    expo, mask = _gla_tables()
    expo2 = np.concatenate([expo, expo], axis=1)
    body = functools.partial(_gla_body, scale=float(hk) ** -0.5, hk=hk, hv=hv)
    side = _SideCast(*side, (b, nc))
    return pl.pallas_call(
        body,
        grid=(b, nc),
        in_specs=[
            pl.BlockSpec((1, CHUNK, dk), lambda i, c: (i, c, q_off // dk)),
            pl.BlockSpec((1, CHUNK, dk), lambda i, c: (i, c, k_off // dk)),
            pl.BlockSpec((1, CHUNK, dv), lambda i, c: (i, c, v_off // dv)),
            pl.BlockSpec((1, CHUNK, dv), lambda i, c: (i, c, r_off // dv)),
            pl.BlockSpec((1, CHUNK, dk), lambda i, c: (i, c, 0)),
            pl.BlockSpec((1, hv), lambda i, c: (0, 0)),
            pl.BlockSpec(expo2.shape, lambda i, c: (0, 0)),
            pl.BlockSpec(mask.shape, lambda i, c: (0, 0, 0)),
            side.in_spec,
        ],
        out_specs=[pl.BlockSpec((1, CHUNK, dv), lambda i, c: (i, c, 0)), side.out_spec],
        out_shape=[jax.ShapeDtypeStruct((b, s, dv), BF16), side.out_shape],
        scratch_shapes=[pltpu.VMEM((GLA_HEADS, hv, hk), F32)],
        compiler_params=_params("arbitrary", "arbitrary"),
        name="gla",
    )(proj3, proj3, proj3, proj3, la3, gn, jnp.asarray(expo2, dtype=BF16), jnp.asarray(mask), side.src)


def _merge_body(ua_ref, ub_ref, wa_ref, wb_ref, ga_ref, gb_ref, bm_ref, o_ref):
    ya = jnp.dot(ua_ref[...], wa_ref[...], preferred_element_type=F32)
    yb = jnp.dot(ub_ref[...], wb_ref[...], preferred_element_type=F32)
    bias = bm_ref[...]
    sa = jax.nn.sigmoid(ga_ref[...].astype(F32) + bias[0:1, :])
    sb = jax.nn.sigmoid(gb_ref[...].astype(F32) + bias[1:2, :])
    o_ref[...] = (sa * ya + sb * yb).astype(o_ref.dtype)


def _merge(ua, ub, wa, wb, proj, gate_col, b_merge, *, bm=1024, bn=512):
    m, kk = ua.shape
    n = wa.shape[1]
    nb = n // bn
    g0 = gate_col // bn
    assert gate_col % bn == 0
    return pl.pallas_call(
        _merge_body,
        grid=(m // bm, nb),
        in_specs=[
            pl.BlockSpec((bm, kk), lambda i, j: (i, 0)),
            pl.BlockSpec((bm, kk), lambda i, j: (i, 0)),
            pl.BlockSpec((kk, bn), lambda i, j: (0, j)),
            pl.BlockSpec((kk, bn), lambda i, j: (0, j)),
            pl.BlockSpec((bm, bn), lambda i, j: (i, g0 + j)),
            pl.BlockSpec((bm, bn), lambda i, j: (i, g0 + nb + j)),
            pl.BlockSpec((2, bn), lambda i, j: (0, j)),
        ],
        out_specs=pl.BlockSpec((bm, bn), lambda i, j: (i, j)),
        out_shape=jax.ShapeDtypeStruct((m, n), BF16),
        compiler_params=_params("parallel", "arbitrary"),
        name="merge",
    )(ua, ub, wa, wb, proj, proj, b_merge)


def kernel(x, ffn1_norm_g, ffn1_w_gate, ffn1_w_up, ffn1_w_down, mix_norm_g, w_in, conv_w, conv_b, w_conv_out,
           w_alpha_up, b_alpha, gla_norm_g, w_gla_out, b_merge, w_mix_out, ffn2_norm_g, ffn2_w_gate, ffn2_w_up,
           ffn2_w_down, final_norm_g):
    batch, seq, d_model = x.shape
    depth = ffn1_w_gate.shape[0]
    d_conv = conv_w.shape[-1]
    rank, d_gla_k = w_alpha_up.shape[-2:]
    d_gla_v = w_gla_out.shape[-2]
    hk, hv = d_gla_k // GLA_HEADS, d_gla_v // GLA_HEADS
    q_off = 3 * d_conv
    k_off = q_off + d_gla_k
    v_off = k_off + d_gla_k
    r_off = v_off + d_gla_v
    a_off = r_off + d_gla_v
    g_off = a_off + rank
    m = batch * seq

    xs = x.reshape(m, d_model)
    for l in range(depth):
        xs, h = _ffn(xs, ffn1_norm_g[l][None], ffn1_w_gate[l].astype(BF16), ffn1_w_up[l].astype(BF16),
                     ffn1_w_down[l].astype(BF16), mix_norm_g[l][None], "x+norm")

        wt = jnp.transpose(w_in[l])
        proj, w2_gate, w2_up = _in_proj(
            h, wt, [(0, a_off), (g_off, 2 * d_model)],
            [(ffn2_w_gate[l], (d_model, IN_PROJ_SIDE_COLS)), (ffn2_w_up[l], (d_model, IN_PROJ_SIDE_COLS))])
        la = _loggate(h, wt, a_off, w_alpha_up[l], b_alpha[l][None])

        proj3 = proj.reshape(batch, seq, proj.shape[1])
        ua = _conv_branch(proj3, conv_w[l], conv_b[l][None], d_conv)
        ub, w2_down = _gla(proj3, la.reshape(batch, seq, d_gla_k), gla_norm_g[l][None],
                           (ffn2_w_down[l], (GLA_SIDE_ROWS, d_model)),
                           q_off=q_off, k_off=k_off, v_off=v_off, r_off=r_off, hk=hk, hv=hv)
        merged = _merge(ua.reshape(m, d_conv), ub.reshape(m, d_gla_v), w_conv_out[l].astype(BF16),
                        w_gla_out[l].astype(BF16), proj, a_off, b_merge[l])
        xs = _matmul_res(merged, w_mix_out[l].astype(BF16), xs)

        emit = "norm" if l == depth - 1 else "x"
        xs = _ffn(xs, ffn2_norm_g[l][None], w2_gate, w2_up, w2_down, final_norm_g[None], emit)
    return xs.reshape(batch, seq, d_model)
```

```python
import functools

import jax
import jax.numpy as jnp
import numpy as np
from jax import lax
from jax.experimental import pallas as pl
from jax.experimental.pallas import tpu as pltpu

F32 = jnp.float32
BF16 = jnp.bfloat16

EPS = 1e-6
FFN_RES = 0.5
CHUNK = 64
CONV_W = 3
GLA_HEADS = 4
GATE_TEMP = 16.0
N_LEVELS = 6
GLA_CHUNKS_PER_STEP = 2
FFN_DOWN_COLS = 512
NORM_ROWS = 16
NORM_UNROLL = 4
IN_PROJ_SIDE_COLS = 128
GLA_SIDE_ROWS = 256

V7X_VMEM_LIMIT_BYTES = 58 * 1024 * 1024
V7X_VMEM_LIMIT_LARGE_BYTES = 62 * 1024 * 1024
LANES = 128
SUBLANES = 8
V7X_MXU_COLS = 256


def _params(*sem, vmem_limit_bytes=V7X_VMEM_LIMIT_BYTES):
    return pltpu.CompilerParams(dimension_semantics=sem, vmem_limit_bytes=vmem_limit_bytes)


def _rms(x, g):
    ms = jnp.mean(x * x, axis=-1, keepdims=True)
    return x * lax.rsqrt(ms + EPS) * g


def _split_bf16(x):
    hi = x.astype(BF16)
    lo = (x - hi.astype(F32)).astype(BF16)
    return hi, lo


def _ffn_body(x_hbm, g_ref, wg_ref, wu_ref, wd_ref, gn_ref, *refs, emit, bm):
    if emit == "x+norm":
        o_hbm, hn_hbm, acc_ref, h_ref, rs_ref, sem_x, sem_o, sem_h = refs
    else:
        o_hbm, acc_ref, h_ref, rs_ref, sem_x, sem_o = refs
    i, f = pl.program_id(0), pl.program_id(1)
    n_i, n_f = pl.num_programs(0), pl.num_programs(1)
    slot = lax.rem(i, 2)
    acc = acc_ref.at[slot]
    d = h_ref.shape[1]
    n_slabs = bm // NORM_ROWS

    def tile(t):
        return pl.ds(pl.multiple_of(t * bm, bm), bm)

    def x_copy(t, s):
        return pltpu.make_async_copy(x_hbm.at[tile(t)], acc_ref.at[s], sem_x.at[s])

    def y_copy(t, s):
        return pltpu.make_async_copy(acc_ref.at[s], o_hbm.at[tile(t)], sem_o.at[s])

    def norm_copy(t):
        return pltpu.make_async_copy(h_ref, hn_hbm.at[tile(t)], sem_h.at[0])

    def rows(s):
        return pl.ds(pl.multiple_of(s * NORM_ROWS, NORM_ROWS), NORM_ROWS)

    @pl.when(f == 0)
    def _():
        @pl.when(i == 0)
        def _():
            x_copy(0, 0).start()

        x_copy(i, slot).wait()
        if emit == "x+norm":
            @pl.when(i > 0)
            def _():
                norm_copy(i - 1).wait()

        g = g_ref[...]

        def slab(s, carry):
            h_ref[rows(s), :] = _rms(acc[rows(s), :], g).astype(BF16)
            return carry

        lax.fori_loop(0, n_slabs, slab, 0, unroll=NORM_UNROLL)

    @pl.when(f == 1)
    def _():
        @pl.when(i > 0)
        def _():
            y_copy(i - 1, 1 - slot).wait()

        @pl.when(i + 1 < n_i)
        def _():
            x_copy(i + 1, 1 - slot).start()

    h = h_ref[...]
    gate = jnp.dot(h, wg_ref[...].astype(BF16), preferred_element_type=F32)
    up = jnp.dot(h, wu_ref[...].astype(BF16), preferred_element_type=F32)
    act = (FFN_RES * (gate * jax.nn.sigmoid(gate) * up)).astype(BF16)
    for n0 in range(0, d, FFN_DOWN_COLS):
        sl = slice(n0, n0 + FFN_DOWN_COLS)
        acc[:, sl] += jnp.dot(act, wd_ref[:, sl].astype(BF16), preferred_element_type=F32)

    @pl.when(f == n_f - 1)
    def _():
        if emit != "x":
            gn = gn_ref[...]

            def stat_slab(s, carry):
                y = acc[rows(s), :]
                rs_ref[rows(s), :] = lax.rsqrt(jnp.mean(y * y, axis=-1, keepdims=True) + EPS)
                return carry

            lax.fori_loop(0, n_slabs, stat_slab, 0, unroll=NORM_UNROLL)

            def norm_slab(s, carry):
                normed = acc[rows(s), :] * rs_ref[rows(s), :] * gn
                if emit == "norm":
                    acc[rows(s), :] = normed
                else:
                    h_ref[rows(s), :] = normed.astype(BF16)
                return carry

            lax.fori_loop(0, n_slabs, norm_slab, 0, unroll=NORM_UNROLL)

        y_copy(i, slot).start()
        if emit == "x+norm":
            norm_copy(i).start()

        @pl.when(i == n_i - 1)
        def _():
            y_copy(i, slot).wait()
            if emit == "x+norm":
                norm_copy(i).wait()


def _ffn(x, g, wg, wu, wd, g_next, emit, *, bm=1024, bf=256):
    m, d = x.shape
    dff = wg.shape[1]
    assert m % bm == 0 and dff // bf >= 2
    vec = pl.BlockSpec((1, d), lambda i, f: (0, 0))
    hbm = pl.BlockSpec(memory_space=pl.ANY)
    out_specs, out_shape = hbm, jax.ShapeDtypeStruct((m, d), F32)
    scratch = [
        pltpu.VMEM((2, bm, d), F32),
        pltpu.VMEM((bm, d), BF16),
        pltpu.VMEM((bm, 1), F32),
        pltpu.SemaphoreType.DMA((2,)),
        pltpu.SemaphoreType.DMA((2,)),
    ]
    if emit == "x+norm":
        out_specs, out_shape = [hbm, hbm], [out_shape, jax.ShapeDtypeStruct((m, d), BF16)]
        scratch.append(pltpu.SemaphoreType.DMA((1,)))
    return pl.pallas_call(
        functools.partial(_ffn_body, emit=emit, bm=bm),
        grid=(m // bm, dff // bf),
        in_specs=[
            hbm,
            vec,
            pl.BlockSpec((d, bf), lambda i, f: (0, f)),
            pl.BlockSpec((d, bf), lambda i, f: (0, f)),
            pl.BlockSpec((bf, d), lambda i, f: (f, 0)),
            vec,
        ],
        out_specs=out_specs,
        out_shape=out_shape,
        scratch_shapes=scratch,
        compiler_params=_params("arbitrary", "arbitrary", vmem_limit_bytes=V7X_VMEM_LIMIT_LARGE_BYTES),
        name="ffn",
    )(x, g, wg, wu, wd, g_next)


class _SideCast:
    def __init__(self, src, block, grid):
        rows, cols = src.shape
        br, bc = block
        assert rows % br == 0 and cols % bc == 0 and (br == rows or bc == cols)
        n_blocks = (rows // br) * (cols // bc)
        assert grid[0] * grid[1] >= n_blocks
        n_inner = grid[1]

        def index(i, j):
            t = jnp.minimum(i * n_inner + j, n_blocks - 1)
            return (t, 0) if bc == cols else (0, t)

        self.src = src
        self.in_spec = pl.BlockSpec(block, index)
        self.out_spec = pl.BlockSpec(block, index)
        self.out_shape = jax.ShapeDtypeStruct(src.shape, BF16)

    @staticmethod
    def step(src_ref, dst_ref):
        dst_ref[...] = src_ref[...].astype(BF16)


def _mm_res_body(a_ref, w_ref, r_ref, o_ref):
    o_ref[...] = r_ref[...] + jnp.dot(a_ref[...], w_ref[...], preferred_element_type=F32)


def _matmul_res(a, w, res, *, bm=1024, bn=512):
    m, k = a.shape
    n = w.shape[1]
    return pl.pallas_call(
        _mm_res_body,
        grid=(m // bm, n // bn),
        in_specs=[
            pl.BlockSpec((bm, k), lambda i, j: (i, 0)),
            pl.BlockSpec((k, bn), lambda i, j: (0, j)),
            pl.BlockSpec((bm, bn), lambda i, j: (i, j)),
        ],
        out_specs=pl.BlockSpec((bm, bn), lambda i, j: (i, j)),
        out_shape=jax.ShapeDtypeStruct((m, n), F32),
        compiler_params=_params("parallel", "arbitrary"),
        name="mix_out",
    )(a, w, res)


def _in_proj_body(a_ref, piece_ref, *refs, n_side):
    side_srcs, o_ref, side_dsts, wbf_ref = refs[:n_side], refs[n_side], refs[n_side + 1:-1], refs[-1]
    jj, i = pl.program_id(0), pl.program_id(1)
    for side_src, side_dst in zip(side_srcs, side_dsts):
        _SideCast.step(side_src, side_dst)

    def stage():
        wbf_ref[lax.rem(jj, 2), i] = piece_ref[...].T.astype(BF16)

    @pl.when(jj == 0)
    def _():
        stage()

    @pl.when(jj > 0)
    def _():
        stage()
        w = wbf_ref.at[lax.rem(jj + 1, 2)]
        a = a_ref[...]
        piece = w.shape[2]
        group = V7X_MXU_COLS // piece
        for p in range(0, w.shape[0], group):
            wp = jnp.concatenate([w[p + q] for q in range(group)], axis=1)
            o_ref[:, p * piece:(p + group) * piece] = jnp.dot(a, wp, preferred_element_type=F32).astype(o_ref.dtype)


def _in_proj(a, wt, groups, sides, *, bm=1024, tile=1024):
    m, k = a.shape
    mi = m // bm
    piece = tile // mi
    assert piece % LANES == 0 and V7X_MXU_COLS % piece == 0
    (off0, rows0), (off1, rows1) = groups
    assert off0 % SUBLANES == 0 and off1 % SUBLANES == 0 and rows0 % tile == 0 and rows1 % tile == 0
    n0 = rows0 // tile
    n_tiles = n0 + rows1 // tile
    grid = (n_tiles + 1, mi)

    def piece_index(jj, i):
        t = jnp.minimum(jj, n_tiles - 1)
        p = jnp.where(jj < n_tiles, i, mi - 1)
        row = jnp.where(t < n0, off0 + t * tile, off1 + (t - n0) * tile) + p * piece
        return pl.multiple_of(row, SUBLANES), 0

    in_specs = [
        pl.BlockSpec((bm, k), lambda jj, i: (jnp.where(jj > 0, i, 0), 0)),
        pl.BlockSpec((pl.Element(piece), pl.Element(k)), piece_index),
    ]
    out_specs = [pl.BlockSpec((bm, tile), lambda jj, i: (jnp.where(jj > 0, i, 0), jnp.maximum(jj - 1, 0)))]
    out_shape = [jax.ShapeDtypeStruct((m, n_tiles * tile), BF16)]
    args = [a, wt]
    for src, block in sides:
        side = _SideCast(src, block, grid)
        in_specs.append(side.in_spec)
        args.append(side.src)
        out_specs.append(side.out_spec)
        out_shape.append(side.out_shape)
    return pl.pallas_call(
        functools.partial(_in_proj_body, n_side=len(sides)),
        grid=grid,
        in_specs=in_specs,
        out_specs=out_specs,
        out_shape=out_shape,
        scratch_shapes=[pltpu.VMEM((2, mi, k, piece), BF16)],
        compiler_params=_params("arbitrary", "arbitrary", vmem_limit_bytes=V7X_VMEM_LIMIT_LARGE_BYTES),
        name="in_proj",
    )(*args)


def _conv_body(cb_ref, cc_ref, cu_ref, w_ref, b_ref, o_ref):
    p = cc_ref[0].astype(F32) * cu_ref[0].astype(F32)
    row = lax.broadcasted_iota(jnp.int32, p.shape, 0)
    p1 = jnp.where(row >= 1, pltpu.roll(p, 1, axis=0), 0.0)
    p2 = jnp.where(row >= 2, pltpu.roll(p, 2, axis=0), 0.0)
    w = w_ref[...]
    conv = b_ref[...] + w[0:1, :] * p2
    conv = conv + w[1:2, :] * p1
    conv = conv + w[2:3, :] * p
    o_ref[0] = (cb_ref[0].astype(F32) * conv).astype(o_ref.dtype)


def _conv_branch(proj3, conv_w, conv_b, d_conv, *, bc=256):
    b, s, _ = proj3.shape
    nb = d_conv // bc
    return pl.pallas_call(
        _conv_body,
        grid=(b, nb),
        in_specs=[
            pl.BlockSpec((1, s, bc), lambda i, j: (i, 0, j)),
            pl.BlockSpec((1, s, bc), lambda i, j: (i, 0, nb + j)),
            pl.BlockSpec((1, s, bc), lambda i, j: (i, 0, 2 * nb + j)),
            pl.BlockSpec((CONV_W, bc), lambda i, j: (0, j)),
            pl.BlockSpec((1, bc), lambda i, j: (0, j)),
        ],
        out_specs=pl.BlockSpec((1, s, bc), lambda i, j: (i, 0, j)),
        out_shape=jax.ShapeDtypeStruct((b, s, d_conv), BF16),
        compiler_params=_params("parallel", "parallel"),
        name="conv",
    )(proj3, proj3, proj3, conv_w, conv_b)


def _loggate_body(h_ref, wl_ref, wup_ref, ba_ref, o_ref):
    nt = (((1,), (1,)), ((), ()))
    a = lax.dot_general(h_ref[...], wl_ref[...].astype(BF16), nt, preferred_element_type=F32)
    a_hi, a_lo = _split_bf16(a)
    w_hi, w_lo = _split_bf16(wup_ref[...])
    z = (jnp.dot(a_hi, w_hi, preferred_element_type=F32) + jnp.dot(a_hi, w_lo, preferred_element_type=F32)
         + jnp.dot(a_lo, w_hi, preferred_element_type=F32)) + ba_ref[...]
    o_ref[...] = (jnp.minimum(z, 0.0) - jnp.log(1.0 + jnp.exp(-jnp.abs(z)))) * (1.0 / GATE_TEMP)


def _loggate(h, wt, row_off, w_up, b_alpha, *, bm=1024):
    m, d = h.shape
    rank, dk = w_up.shape
    return pl.pallas_call(
        _loggate_body,
        grid=(m // bm,),
        in_specs=[
            pl.BlockSpec((bm, d), lambda i: (i, 0)),
            pl.BlockSpec((pl.Element(rank), pl.Element(d)), lambda i: (row_off, 0)),
            pl.BlockSpec((rank, dk), lambda i: (0, 0)),
            pl.BlockSpec((1, dk), lambda i: (0, 0)),
        ],
        out_specs=pl.BlockSpec((bm, dk), lambda i: (i, 0)),
        out_shape=jax.ShapeDtypeStruct((m, dk), F32),
        compiler_params=_params("parallel"),
        name="loggate",
    )(h, wt, w_up, b_alpha)


def _gla_tables():
    c = CHUNK
    i = np.arange(c)[:, None]
    t = np.arange(c)[None, :]
    blocks = [(t <= i), (t > i)]
    masks = []
    for lvl in range(N_LEVELS):
        p = c >> lvl
        half = p // 2
        mid = (i // p) * p + half
        upper = i >= mid
        blocks.append(np.where(upper, (t > mid) & (t <= i), (t > i) & (t <= mid)))
        j = t
        same_parent = (i // p) == (j // p)
        masks.append(same_parent & (((i % p) >= half) != ((j % p) >= half)))
    expo = np.concatenate(blocks, axis=0).astype(np.float32)
    mask = np.stack(masks).astype(np.float32)
    return expo, mask


def _gla_body(q_ref, k_ref, v_ref, r_ref, la_ref, gn_ref, expo_ref, mask_ref, side_src, o_ref, side_dst, state_ref, *,
              scale, hk, hv):
    c = CHUNK
    _SideCast.step(side_src, side_dst)
    nt = (((1,), (1,)), ((), ()))
    tn = (((0,), (0,)), ((), ()))

    @pl.when(pl.program_id(1) == 0)
    def _():
        state_ref[...] = jnp.zeros_like(state_ref)

    row = lax.broadcasted_iota(jnp.int32, (c, c), 0)
    col = lax.broadcasted_iota(jnp.int32, (c, c), 1)
    eye = row == col
    for h in range(GLA_HEADS):
        ks = slice(h * hk, (h + 1) * hk)
        vs = slice(h * hv, (h + 1) * hv)
        state = state_ref[h]
        for cc in range(GLA_CHUNKS_PER_STEP):
            rs = slice(cc * c, (cc + 1) * c)
            q = q_ref[0, rs, ks].astype(F32) * scale
            k = k_ref[0, rs, ks].astype(F32)
            v = v_ref[0, rs, vs]

            hi, lo = _split_bf16(la_ref[0, rs, ks])
            expo = jnp.dot(expo_ref[...], jnp.concatenate([hi, lo], axis=0), preferred_element_type=F32)
            e_all = jnp.exp(expo)
            e_l = e_all[0:c]
            e_rest = e_all[c:2 * c]

            scores = jnp.where(eye, jnp.sum(q * k, axis=-1, keepdims=True), 0.0)
            for lvl in range(N_LEVELS):
                e = e_all[(2 + lvl) * c:(3 + lvl) * c]
                p = lax.dot_general((q * e).astype(BF16), (k * e).astype(BF16), nt, preferred_element_type=F32)
                scores = scores + mask_ref[lvl] * p

            o = jnp.dot(scores.astype(BF16), v, preferred_element_type=F32)
            o = o + lax.dot_general((q * e_l).astype(BF16), state.astype(BF16), nt, preferred_element_type=F32)

            k_dec = (k * e_rest).astype(BF16)
            state = state * e_l[c - 1:c, :] + lax.dot_general(v, k_dec, tn, preferred_element_type=F32)

            r = r_ref[0, rs, vs].astype(F32)
            o_ref[0, rs, vs] = (_rms(o, gn_ref[...]) * (r * jax.nn.sigmoid(r))).astype(o_ref.dtype)
        state_ref[h] = state


def _gla(proj3, la3, gn, side, *, q_off, k_off, v_off, r_off, hk, hv):
    b, s, _ = proj3.shape
    rows = CHUNK * GLA_CHUNKS_PER_STEP
    nc = s // rows
    dk, dv = GLA_HEADS * hk, GLA_HEADS * hv
    expo, mask = _gla_tables()
    expo2 = np.concatenate([expo, expo], axis=1)
    body = functools.partial(_gla_body, scale=float(hk) ** -0.5, hk=hk, hv=hv)
    side = _SideCast(*side, (b, nc))
    return pl.pallas_call(
        body,
        grid=(b, nc),
        in_specs=[
            pl.BlockSpec((1, rows, dk), lambda i, c: (i, c, q_off // dk)),
            pl.BlockSpec((1, rows, dk), lambda i, c: (i, c, k_off // dk)),
            pl.BlockSpec((1, rows, dv), lambda i, c: (i, c, v_off // dv)),
            pl.BlockSpec((1, rows, dv), lambda i, c: (i, c, r_off // dv)),
            pl.BlockSpec((1, rows, dk), lambda i, c: (i, c, 0)),
            pl.BlockSpec((1, hv), lambda i, c: (0, 0)),
            pl.BlockSpec(expo2.shape, lambda i, c: (0, 0)),
            pl.BlockSpec(mask.shape, lambda i, c: (0, 0, 0)),
            side.in_spec,
        ],
        out_specs=[pl.BlockSpec((1, rows, dv), lambda i, c: (i, c, 0)), side.out_spec],
        out_shape=[jax.ShapeDtypeStruct((b, s, dv), BF16), side.out_shape],
        scratch_shapes=[pltpu.VMEM((GLA_HEADS, hv, hk), F32)],
        compiler_params=_params("arbitrary", "arbitrary"),
        name="gla",
    )(proj3, proj3, proj3, proj3, la3, gn, jnp.asarray(expo2, dtype=BF16), jnp.asarray(mask), side.src)


def _merge_body(ua_ref, ub_ref, wa_ref, wb_ref, ga_ref, gb_ref, bm_ref, o_ref):
    ya = jnp.dot(ua_ref[...], wa_ref[...], preferred_element_type=F32)
    yb = jnp.dot(ub_ref[...], wb_ref[...], preferred_element_type=F32)
    bias = bm_ref[...]
    sa = jax.nn.sigmoid(ga_ref[...].astype(F32) + bias[0:1, :])
    sb = jax.nn.sigmoid(gb_ref[...].astype(F32) + bias[1:2, :])
    o_ref[...] = (sa * ya + sb * yb).astype(o_ref.dtype)


def _merge(ua, ub, wa, wb, proj, gate_col, b_merge, *, bm=1024, bn=512):
    m, kk = ua.shape
    n = wa.shape[1]
    nb = n // bn
    g0 = gate_col // bn
    assert gate_col % bn == 0
    return pl.pallas_call(
        _merge_body,
        grid=(m // bm, nb),
        in_specs=[
            pl.BlockSpec((bm, kk), lambda i, j: (i, 0)),
            pl.BlockSpec((bm, kk), lambda i, j: (i, 0)),
            pl.BlockSpec((kk, bn), lambda i, j: (0, j)),
            pl.BlockSpec((kk, bn), lambda i, j: (0, j)),
            pl.BlockSpec((bm, bn), lambda i, j: (i, g0 + j)),
            pl.BlockSpec((bm, bn), lambda i, j: (i, g0 + nb + j)),
            pl.BlockSpec((2, bn), lambda i, j: (0, j)),
        ],
        out_specs=pl.BlockSpec((bm, bn), lambda i, j: (i, j)),
        out_shape=jax.ShapeDtypeStruct((m, n), BF16),
        compiler_params=_params("parallel", "arbitrary"),
        name="merge",
    )(ua, ub, wa, wb, proj, proj, b_merge)


def kernel(x, ffn1_norm_g, ffn1_w_gate, ffn1_w_up, ffn1_w_down, mix_norm_g, w_in, conv_w, conv_b, w_conv_out,
           w_alpha_up, b_alpha, gla_norm_g, w_gla_out, b_merge, w_mix_out, ffn2_norm_g, ffn2_w_gate, ffn2_w_up,
           ffn2_w_down, final_norm_g):
    batch, seq, d_model = x.shape
    depth = ffn1_w_gate.shape[0]
    d_conv = conv_w.shape[-1]
    rank, d_gla_k = w_alpha_up.shape[-2:]
    d_gla_v = w_gla_out.shape[-2]
    hk, hv = d_gla_k // GLA_HEADS, d_gla_v // GLA_HEADS
    q_off = 3 * d_conv
    k_off = q_off + d_gla_k
    v_off = k_off + d_gla_k
    r_off = v_off + d_gla_v
    a_off = r_off + d_gla_v
    g_off = a_off + rank
    m = batch * seq

    xs = x.reshape(m, d_model)
    for l in range(depth):
        xs, h = _ffn(xs, ffn1_norm_g[l][None], ffn1_w_gate[l].astype(BF16), ffn1_w_up[l], ffn1_w_down[l],
                     mix_norm_g[l][None], "x+norm")

        wt = jnp.transpose(w_in[l])
        proj, w2_gate, w2_up = _in_proj(
            h, wt, [(0, a_off), (g_off, 2 * d_model)],
            [(ffn2_w_gate[l], (d_model, IN_PROJ_SIDE_COLS)), (ffn2_w_up[l], (d_model, IN_PROJ_SIDE_COLS))])
        la = _loggate(h, wt, a_off, w_alpha_up[l], b_alpha[l][None])

        proj3 = proj.reshape(batch, seq, proj.shape[1])
        ua = _conv_branch(proj3, conv_w[l], conv_b[l][None], d_conv)
        ub, w2_down = _gla(proj3, la.reshape(batch, seq, d_gla_k), gla_norm_g[l][None],
                           (ffn2_w_down[l], (GLA_SIDE_ROWS, d_model)),
                           q_off=q_off, k_off=k_off, v_off=v_off, r_off=r_off, hk=hk, hv=hv)
        merged = _merge(ua.reshape(m, d_conv), ub.reshape(m, d_gla_v), w_conv_out[l].astype(BF16),
                        w_gla_out[l].astype(BF16), proj, a_off, b_merge[l])
        xs = _matmul_res(merged, w_mix_out[l].astype(BF16), xs)

        emit = "norm" if l == depth - 1 else "x"
        xs = _ffn(xs, ffn2_norm_g[l][None], w2_gate, w2_up, w2_down, final_norm_g[None], emit)
    return xs.reshape(batch, seq, d_model)
```

```python
import functools

import jax
import jax.numpy as jnp
import numpy as np
from jax import lax
from jax.experimental import pallas as pl
from jax.experimental.pallas import tpu as pltpu

F32 = jnp.float32
BF16 = jnp.bfloat16

EPS = 1e-6
FFN_RES = 0.5
CHUNK = 64
CONV_W = 3
GLA_HEADS = 4
GATE_TEMP = 16.0
N_LEVELS = 6
GLA_CHUNKS_PER_STEP = 2
FFN_DOWN_COLS = 512
NORM_ROWS = 16
NORM_UNROLL = 4
IN_PROJ_SIDE_COLS = 128
GLA_SIDE_ROWS = 256

V7X_VMEM_LIMIT_BYTES = 58 * 1024 * 1024
V7X_VMEM_LIMIT_LARGE_BYTES = 62 * 1024 * 1024
LANES = 128
SUBLANES = 8
V7X_MXU_COLS = 256


def _params(*sem, vmem_limit_bytes=V7X_VMEM_LIMIT_BYTES):
    return pltpu.CompilerParams(dimension_semantics=sem, vmem_limit_bytes=vmem_limit_bytes)


def _rms(x, g):
    ms = jnp.mean(x * x, axis=-1, keepdims=True)
    return x * lax.rsqrt(ms + EPS) * g


def _split_bf16(x):
    hi = x.astype(BF16)
    lo = (x - hi.astype(F32)).astype(BF16)
    return hi, lo


def _ffn_body(x_hbm, g_ref, wg_ref, wu_ref, wd_ref, gn_ref, *refs, emit, bm):
    if emit == "x+norm":
        o_hbm, hn_hbm, acc_ref, h_ref, rs_ref, sem_x, sem_o, sem_h = refs
    else:
        o_hbm, acc_ref, h_ref, rs_ref, sem_x, sem_o = refs
    i, f = pl.program_id(0), pl.program_id(1)
    n_i, n_f = pl.num_programs(0), pl.num_programs(1)
    slot = lax.rem(i, 2)
    acc = acc_ref.at[slot]
    d = h_ref.shape[1]
    n_slabs = bm // NORM_ROWS

    def tile(t):
        return pl.ds(pl.multiple_of(t * bm, bm), bm)

    def x_copy(t, s):
        return pltpu.make_async_copy(x_hbm.at[tile(t)], acc_ref.at[s], sem_x.at[s])

    def y_copy(t, s):
        return pltpu.make_async_copy(acc_ref.at[s], o_hbm.at[tile(t)], sem_o.at[s])

    def norm_copy(t):
        return pltpu.make_async_copy(h_ref, hn_hbm.at[tile(t)], sem_h.at[0])

    def rows(s):
        return pl.ds(pl.multiple_of(s * NORM_ROWS, NORM_ROWS), NORM_ROWS)

    @pl.when(f == 0)
    def _():
        @pl.when(i == 0)
        def _():
            x_copy(0, 0).start()

        x_copy(i, slot).wait()
        if emit == "x+norm":
            @pl.when(i > 0)
            def _():
                norm_copy(i - 1).wait()

        g = g_ref[...]

        def slab(s, carry):
            h_ref[rows(s), :] = _rms(acc[rows(s), :], g).astype(BF16)
            return carry

        lax.fori_loop(0, n_slabs, slab, 0, unroll=NORM_UNROLL)

    @pl.when(f == 1)
    def _():
        @pl.when(i > 0)
        def _():
            y_copy(i - 1, 1 - slot).wait()

        @pl.when(i + 1 < n_i)
        def _():
            x_copy(i + 1, 1 - slot).start()

    h = h_ref[...]
    gate = jnp.dot(h, wg_ref[...].astype(BF16), preferred_element_type=F32)
    up = jnp.dot(h, wu_ref[...].astype(BF16), preferred_element_type=F32)
    act = (FFN_RES * (gate * jax.nn.sigmoid(gate) * up)).astype(BF16)
    for n0 in range(0, d, FFN_DOWN_COLS):
        sl = slice(n0, n0 + FFN_DOWN_COLS)
        acc[:, sl] += jnp.dot(act, wd_ref[:, sl].astype(BF16), preferred_element_type=F32)

    @pl.when(f == n_f - 1)
    def _():
        if emit != "x":
            gn = gn_ref[...]

            def stat_slab(s, carry):
                y = acc[rows(s), :]
                rs_ref[rows(s), :] = lax.rsqrt(jnp.mean(y * y, axis=-1, keepdims=True) + EPS)
                return carry

            lax.fori_loop(0, n_slabs, stat_slab, 0, unroll=NORM_UNROLL)

            def norm_slab(s, carry):
                normed = acc[rows(s), :] * rs_ref[rows(s), :] * gn
                if emit == "norm":
                    acc[rows(s), :] = normed
                else:
                    h_ref[rows(s), :] = normed.astype(BF16)
                return carry

            lax.fori_loop(0, n_slabs, norm_slab, 0, unroll=NORM_UNROLL)

        if emit == "x+norm":
            norm_copy(i).start()
        y_copy(i, slot).start()

        @pl.when(i == n_i - 1)
        def _():
            y_copy(i, slot).wait()
            if emit == "x+norm":
                norm_copy(i).wait()


def _ffn(x, g, wg, wu, wd, g_next, emit, *, bm=1024, bf=256):
    m, d = x.shape
    dff = wg.shape[1]
    assert m % bm == 0 and dff // bf >= 2
    vec = pl.BlockSpec((1, d), lambda i, f: (0, 0))
    hbm = pl.BlockSpec(memory_space=pl.ANY)
    out_specs, out_shape = hbm, jax.ShapeDtypeStruct((m, d), F32)
    scratch = [
        pltpu.VMEM((2, bm, d), F32),
        pltpu.VMEM((bm, d), BF16),
        pltpu.VMEM((bm, 1), F32),
        pltpu.SemaphoreType.DMA((2,)),
        pltpu.SemaphoreType.DMA((2,)),
    ]
    if emit == "x+norm":
        out_specs, out_shape = [hbm, hbm], [out_shape, jax.ShapeDtypeStruct((m, d), BF16)]
        scratch.append(pltpu.SemaphoreType.DMA((1,)))
    return pl.pallas_call(
        functools.partial(_ffn_body, emit=emit, bm=bm),
        grid=(m // bm, dff // bf),
        in_specs=[
            hbm,
            vec,
            pl.BlockSpec((d, bf), lambda i, f: (0, f)),
            pl.BlockSpec((d, bf), lambda i, f: (0, f)),
            pl.BlockSpec((bf, d), lambda i, f: (f, 0)),
            vec,
        ],
        out_specs=out_specs,
        out_shape=out_shape,
        scratch_shapes=scratch,
        compiler_params=_params("arbitrary", "arbitrary", vmem_limit_bytes=V7X_VMEM_LIMIT_LARGE_BYTES),
        name="ffn",
    )(x, g, wg, wu, wd, g_next)


class _SideCast:
    def __init__(self, src, block, grid):
        rows, cols = src.shape
        br, bc = block
        assert rows % br == 0 and cols % bc == 0 and (br == rows or bc == cols)
        n_blocks = (rows // br) * (cols // bc)
        assert grid[0] * grid[1] >= n_blocks
        n_inner = grid[1]

        def index(i, j):
            t = jnp.minimum(i * n_inner + j, n_blocks - 1)
            return (t, 0) if bc == cols else (0, t)

        self.src = src
        self.in_spec = pl.BlockSpec(block, index)
        self.out_spec = pl.BlockSpec(block, index)
        self.out_shape = jax.ShapeDtypeStruct(src.shape, BF16)

    @staticmethod
    def step(src_ref, dst_ref):
        dst_ref[...] = src_ref[...].astype(BF16)


def _mm_res_body(a_ref, w_ref, r_ref, o_ref):
    o_ref[...] = r_ref[...] + jnp.dot(a_ref[...], w_ref[...], preferred_element_type=F32)


def _matmul_res(a, w, res, *, bm=1024, bn=512):
    m, k = a.shape
    n = w.shape[1]
    return pl.pallas_call(
        _mm_res_body,
        grid=(m // bm, n // bn),
        in_specs=[
            pl.BlockSpec((bm, k), lambda i, j: (i, 0)),
            pl.BlockSpec((k, bn), lambda i, j: (0, j)),
            pl.BlockSpec((bm, bn), lambda i, j: (i, j)),
        ],
        out_specs=pl.BlockSpec((bm, bn), lambda i, j: (i, j)),
        out_shape=jax.ShapeDtypeStruct((m, n), F32),
        compiler_params=_params("parallel", "arbitrary"),
        name="mix_out",
    )(a, w, res)


def _in_proj_body(a_ref, piece_ref, *refs, n_side):
    side_srcs, o_ref, side_dsts, wbf_ref = refs[:n_side], refs[n_side], refs[n_side + 1:-1], refs[-1]
    jj, i = pl.program_id(0), pl.program_id(1)
    for side_src, side_dst in zip(side_srcs, side_dsts):
        _SideCast.step(side_src, side_dst)

    def stage():
        wbf_ref[lax.rem(jj, 2), i] = piece_ref[...].T.astype(BF16)

    @pl.when(jj == 0)
    def _():
        stage()

    @pl.when(jj > 0)
    def _():
        stage()
        w = wbf_ref.at[lax.rem(jj + 1, 2)]
        a = a_ref[...]
        piece = w.shape[2]
        group = V7X_MXU_COLS // piece
        for p in range(0, w.shape[0], group):
            wp = jnp.concatenate([w[p + q] for q in range(group)], axis=1)
            o_ref[:, p * piece:(p + group) * piece] = jnp.dot(a, wp, preferred_element_type=F32).astype(o_ref.dtype)


def _in_proj(a, wt, groups, sides, *, bm=1024, tile=1024):
    m, k = a.shape
    mi = m // bm
    piece = tile // mi
    assert piece % LANES == 0 and V7X_MXU_COLS % piece == 0
    (off0, rows0), (off1, rows1) = groups
    assert off0 % SUBLANES == 0 and off1 % SUBLANES == 0 and rows0 % tile == 0 and rows1 % tile == 0
    n0 = rows0 // tile
    n_tiles = n0 + rows1 // tile
    grid = (n_tiles + 1, mi)

    def piece_index(jj, i):
        t = jnp.minimum(jj, n_tiles - 1)
        p = jnp.where(jj < n_tiles, i, mi - 1)
        row = jnp.where(t < n0, off0 + t * tile, off1 + (t - n0) * tile) + p * piece
        return pl.multiple_of(row, SUBLANES), 0

    in_specs = [
        pl.BlockSpec((bm, k), lambda jj, i: (jnp.where(jj > 0, i, 0), 0)),
        pl.BlockSpec((pl.Element(piece), pl.Element(k)), piece_index),
    ]
    out_specs = [pl.BlockSpec((bm, tile), lambda jj, i: (jnp.where(jj > 0, i, 0), jnp.maximum(jj - 1, 0)))]
    out_shape = [jax.ShapeDtypeStruct((m, n_tiles * tile), BF16)]
    args = [a, wt]
    for src, block in sides:
        side = _SideCast(src, block, grid)
        in_specs.append(side.in_spec)
        args.append(side.src)
        out_specs.append(side.out_spec)
        out_shape.append(side.out_shape)
    return pl.pallas_call(
        functools.partial(_in_proj_body, n_side=len(sides)),
        grid=grid,
        in_specs=in_specs,
        out_specs=out_specs,
        out_shape=out_shape,
        scratch_shapes=[pltpu.VMEM((2, mi, k, piece), BF16)],
        compiler_params=_params("arbitrary", "arbitrary", vmem_limit_bytes=V7X_VMEM_LIMIT_LARGE_BYTES),
        name="in_proj",
    )(*args)


def _conv_body(cb_ref, cc_ref, cu_ref, w_ref, b_ref, o_ref):
    p = cc_ref[0].astype(F32) * cu_ref[0].astype(F32)
    row = lax.broadcasted_iota(jnp.int32, p.shape, 0)
    p1 = jnp.where(row >= 1, pltpu.roll(p, 1, axis=0), 0.0)
    p2 = jnp.where(row >= 2, pltpu.roll(p, 2, axis=0), 0.0)
    w = w_ref[...]
    conv = b_ref[...] + w[0:1, :] * p2
    conv = conv + w[1:2, :] * p1
    conv = conv + w[2:3, :] * p
    o_ref[0] = (cb_ref[0].astype(F32) * conv).astype(o_ref.dtype)


def _conv_branch(proj3, conv_w, conv_b, d_conv, *, bc=256):
    b, s, _ = proj3.shape
    nb = d_conv // bc
    return pl.pallas_call(
        _conv_body,
        grid=(b, nb),
        in_specs=[
            pl.BlockSpec((1, s, bc), lambda i, j: (i, 0, j)),
            pl.BlockSpec((1, s, bc), lambda i, j: (i, 0, nb + j)),
            pl.BlockSpec((1, s, bc), lambda i, j: (i, 0, 2 * nb + j)),
            pl.BlockSpec((CONV_W, bc), lambda i, j: (0, j)),
            pl.BlockSpec((1, bc), lambda i, j: (0, j)),
        ],
        out_specs=pl.BlockSpec((1, s, bc), lambda i, j: (i, 0, j)),
        out_shape=jax.ShapeDtypeStruct((b, s, d_conv), BF16),
        compiler_params=_params("parallel", "parallel"),
        name="conv",
    )(proj3, proj3, proj3, conv_w, conv_b)


def _loggate_body(h_ref, wl_ref, wup_ref, ba_ref, o_ref):
    nt = (((1,), (1,)), ((), ()))
    a = lax.dot_general(h_ref[...], wl_ref[...].astype(BF16), nt, preferred_element_type=F32)
    a_hi, a_lo = _split_bf16(a)
    w_hi, w_lo = _split_bf16(wup_ref[...])
    z = (jnp.dot(a_hi, w_hi, preferred_element_type=F32) + jnp.dot(a_hi, w_lo, preferred_element_type=F32)
         + jnp.dot(a_lo, w_hi, preferred_element_type=F32)) + ba_ref[...]
    o_ref[...] = (jnp.minimum(z, 0.0) - jnp.log(1.0 + jnp.exp(-jnp.abs(z)))) * (1.0 / GATE_TEMP)


def _loggate(h, wt, row_off, w_up, b_alpha, *, bm=1024):
    m, d = h.shape
    rank, dk = w_up.shape
    return pl.pallas_call(
        _loggate_body,
        grid=(m // bm,),
        in_specs=[
            pl.BlockSpec((bm, d), lambda i: (i, 0)),
            pl.BlockSpec((pl.Element(rank), pl.Element(d)), lambda i: (row_off, 0)),
            pl.BlockSpec((rank, dk), lambda i: (0, 0)),
            pl.BlockSpec((1, dk), lambda i: (0, 0)),
        ],
        out_specs=pl.BlockSpec((bm, dk), lambda i: (i, 0)),
        out_shape=jax.ShapeDtypeStruct((m, dk), F32),
        compiler_params=_params("parallel"),
        name="loggate",
    )(h, wt, w_up, b_alpha)


def _gla_tables():
    c = CHUNK
    i = np.arange(c)[:, None]
    t = np.arange(c)[None, :]
    blocks = [(t <= i), (t > i)]
    masks = []
    for lvl in range(N_LEVELS):
        p = c >> lvl
        half = p // 2
        mid = (i // p) * p + half
        upper = i >= mid
        blocks.append(np.where(upper, (t > mid) & (t <= i), (t > i) & (t <= mid)))
        j = t
        same_parent = (i // p) == (j // p)
        masks.append(same_parent & (((i % p) >= half) != ((j % p) >= half)))
    expo = np.concatenate(blocks, axis=0).astype(np.float32)
    mask = np.stack(masks).astype(np.float32)
    return expo, mask


def _gla_body(q_ref, k_ref, v_ref, r_ref, la_ref, gn_ref, expo_ref, mask_ref, *refs, scale, hk, hv, n_side):
    c = CHUNK
    side_srcs, o_ref, side_dsts, state_ref = refs[:n_side], refs[n_side], refs[n_side + 1:-1], refs[-1]
    for side_src, side_dst in zip(side_srcs, side_dsts):
        _SideCast.step(side_src, side_dst)
    nt = (((1,), (1,)), ((), ()))
    tn = (((0,), (0,)), ((), ()))

    @pl.when(pl.program_id(1) == 0)
    def _():
        state_ref[...] = jnp.zeros_like(state_ref)

    row = lax.broadcasted_iota(jnp.int32, (c, c), 0)
    col = lax.broadcasted_iota(jnp.int32, (c, c), 1)
    eye = row == col
    for h in range(GLA_HEADS):
        ks = slice(h * hk, (h + 1) * hk)
        vs = slice(h * hv, (h + 1) * hv)
        state = state_ref[h]
        for cc in range(GLA_CHUNKS_PER_STEP):
            rs = slice(cc * c, (cc + 1) * c)
            q = q_ref[0, rs, ks].astype(F32) * scale
            k = k_ref[0, rs, ks].astype(F32)
            v = v_ref[0, rs, vs]

            hi, lo = _split_bf16(la_ref[0, rs, ks])
            expo = jnp.dot(expo_ref[...], jnp.concatenate([hi, lo], axis=0), preferred_element_type=F32)
            e_all = jnp.exp(expo)
            e_l = e_all[0:c]
            e_rest = e_all[c:2 * c]

            scores = jnp.where(eye, jnp.sum(q * k, axis=-1, keepdims=True), 0.0)
            for lvl in range(N_LEVELS):
                e = e_all[(2 + lvl) * c:(3 + lvl) * c]
                p = lax.dot_general((q * e).astype(BF16), (k * e).astype(BF16), nt, preferred_element_type=F32)
                scores = scores + mask_ref[lvl] * p

            o = jnp.dot(scores.astype(BF16), v, preferred_element_type=F32)
            o = o + lax.dot_general((q * e_l).astype(BF16), state.astype(BF16), nt, preferred_element_type=F32)

            k_dec = (k * e_rest).astype(BF16)
            state = state * e_l[c - 1:c, :] + lax.dot_general(v, k_dec, tn, preferred_element_type=F32)

            r = r_ref[0, rs, vs].astype(F32)
            o_ref[0, rs, vs] = (_rms(o, gn_ref[...]) * (r * jax.nn.sigmoid(r))).astype(o_ref.dtype)
        state_ref[h] = state


def _gla(proj3, la3, gn, sides, *, q_off, k_off, v_off, r_off, hk, hv):
    b, s, _ = proj3.shape
    rows = CHUNK * GLA_CHUNKS_PER_STEP
    nc = s // rows
    dk, dv = GLA_HEADS * hk, GLA_HEADS * hv
    expo, mask = _gla_tables()
    expo2 = np.concatenate([expo, expo], axis=1)
    body = functools.partial(_gla_body, scale=float(hk) ** -0.5, hk=hk, hv=hv, n_side=len(sides))
    sides = [_SideCast(src, block, (b, nc)) for src, block in sides]
    return pl.pallas_call(
        body,
        grid=(b, nc),
        in_specs=[
            pl.BlockSpec((1, rows, dk), lambda i, c: (i, c, q_off // dk)),
            pl.BlockSpec((1, rows, dk), lambda i, c: (i, c, k_off // dk)),
            pl.BlockSpec((1, rows, dv), lambda i, c: (i, c, v_off // dv)),
            pl.BlockSpec((1, rows, dv), lambda i, c: (i, c, r_off // dv)),
            pl.BlockSpec((1, rows, dk), lambda i, c: (i, c, 0)),
            pl.BlockSpec((1, hv), lambda i, c: (0, 0)),
            pl.BlockSpec(expo2.shape, lambda i, c: (0, 0)),
            pl.BlockSpec(mask.shape, lambda i, c: (0, 0, 0)),
            *[side.in_spec for side in sides],
        ],
        out_specs=[pl.BlockSpec((1, rows, dv), lambda i, c: (i, c, 0)), *[side.out_spec for side in sides]],
        out_shape=[jax.ShapeDtypeStruct((b, s, dv), BF16), *[side.out_shape for side in sides]],
        scratch_shapes=[pltpu.VMEM((GLA_HEADS, hv, hk), F32)],
        compiler_params=_params("arbitrary", "arbitrary"),
        name="gla",
    )(proj3, proj3, proj3, proj3, la3, gn, jnp.asarray(expo2, dtype=BF16), jnp.asarray(mask),
      *[side.src for side in sides])


def _merge_body(ua_ref, ub_ref, wa_ref, wb_ref, ga_ref, gb_ref, bm_ref, o_ref):
    ya = jnp.dot(ua_ref[...], wa_ref[...], preferred_element_type=F32)
    yb = jnp.dot(ub_ref[...], wb_ref[...], preferred_element_type=F32)
    bias = bm_ref[...]
    sa = jax.nn.sigmoid(ga_ref[...].astype(F32) + bias[0:1, :])
    sb = jax.nn.sigmoid(gb_ref[...].astype(F32) + bias[1:2, :])
    o_ref[...] = (sa * ya + sb * yb).astype(o_ref.dtype)


def _merge(ua, ub, wa, wb, proj, gate_col, b_merge, *, bm=1024, bn=512):
    m, kk = ua.shape
    n = wa.shape[1]
    nb = n // bn
    g0 = gate_col // bn
    assert gate_col % bn == 0
    return pl.pallas_call(
        _merge_body,
        grid=(m // bm, nb),
        in_specs=[
            pl.BlockSpec((bm, kk), lambda i, j: (i, 0)),
            pl.BlockSpec((bm, kk), lambda i, j: (i, 0)),
            pl.BlockSpec((kk, bn), lambda i, j: (0, j)),
            pl.BlockSpec((kk, bn), lambda i, j: (0, j)),
            pl.BlockSpec((bm, bn), lambda i, j: (i, g0 + j)),
            pl.BlockSpec((bm, bn), lambda i, j: (i, g0 + nb + j)),
            pl.BlockSpec((2, bn), lambda i, j: (0, j)),
        ],
        out_specs=pl.BlockSpec((bm, bn), lambda i, j: (i, j)),
        out_shape=jax.ShapeDtypeStruct((m, n), BF16),
        compiler_params=_params("parallel", "arbitrary"),
        name="merge",
    )(ua, ub, wa, wb, proj, proj, b_merge)


def kernel(x, ffn1_norm_g, ffn1_w_gate, ffn1_w_up, ffn1_w_down, mix_norm_g, w_in, conv_w, conv_b, w_conv_out,
           w_alpha_up, b_alpha, gla_norm_g, w_gla_out, b_merge, w_mix_out, ffn2_norm_g, ffn2_w_gate, ffn2_w_up,
           ffn2_w_down, final_norm_g):
    batch, seq, d_model = x.shape
    depth = ffn1_w_gate.shape[0]
    d_conv = conv_w.shape[-1]
    rank, d_gla_k = w_alpha_up.shape[-2:]
    d_gla_v = w_gla_out.shape[-2]
    hk, hv = d_gla_k // GLA_HEADS, d_gla_v // GLA_HEADS
    q_off = 3 * d_conv
    k_off = q_off + d_gla_k
    v_off = k_off + d_gla_k
    r_off = v_off + d_gla_v
    a_off = r_off + d_gla_v
    g_off = a_off + rank
    m = batch * seq

    xs = x.reshape(m, d_model)
    for l in range(depth):
        xs, h = _ffn(xs, ffn1_norm_g[l][None], ffn1_w_gate[l].astype(BF16), ffn1_w_up[l], ffn1_w_down[l],
                     mix_norm_g[l][None], "x+norm")

        wt = jnp.transpose(w_in[l])
        proj, w2_gate, w2_up = _in_proj(
            h, wt, [(0, a_off), (g_off, 2 * d_model)],
            [(ffn2_w_gate[l], (d_model, IN_PROJ_SIDE_COLS)), (ffn2_w_up[l], (d_model, IN_PROJ_SIDE_COLS))])
        la = _loggate(h, wt, a_off, w_alpha_up[l], b_alpha[l][None])

        proj3 = proj.reshape(batch, seq, proj.shape[1])
        ua = _conv_branch(proj3, conv_w[l], conv_b[l][None], d_conv)
        gla_steps = batch * (seq // (CHUNK * GLA_CHUNKS_PER_STEP))
        ub, w2_down, w_mix, w_conv, w_gla = _gla(
            proj3, la.reshape(batch, seq, d_gla_k), gla_norm_g[l][None],
            [(ffn2_w_down[l], (GLA_SIDE_ROWS, d_model)),
             (w_mix_out[l], (d_model // gla_steps, d_model)),
             (w_conv_out[l], (d_conv // gla_steps, d_model)),
             (w_gla_out[l], (d_gla_v // gla_steps, d_model))],
            q_off=q_off, k_off=k_off, v_off=v_off, r_off=r_off, hk=hk, hv=hv)
        merged = _merge(ua.reshape(m, d_conv), ub.reshape(m, d_gla_v), w_conv, w_gla, proj, a_off, b_merge[l])
        xs = _matmul_res(merged, w_mix, xs)

        emit = "norm" if l == depth - 1 else "x"
        xs = _ffn(xs, ffn2_norm_g[l][None], w2_gate, w2_up, w2_down, final_norm_g[None], emit)
    return xs.reshape(batch, seq, d_model)
```

```python
import functools

import jax
import jax.numpy as jnp
import numpy as np
from jax import lax
from jax.experimental import pallas as pl
from jax.experimental.pallas import tpu as pltpu

F32 = jnp.float32
BF16 = jnp.bfloat16

EPS = 1e-6
FFN_RES = 0.5
CHUNK = 64
CONV_W = 3
GLA_HEADS = 4
GATE_TEMP = 16.0
N_LEVELS = 6
GLA_CHUNKS_PER_STEP = 2
FFN_DOWN_COLS = 512
NORM_ROWS = 16
NORM_UNROLL = 8
IN_PROJ_SIDE_COLS = 128
GLA_SIDE_ROWS = 256

V7X_VMEM_LIMIT_BYTES = 58 * 1024 * 1024
V7X_VMEM_LIMIT_LARGE_BYTES = 62 * 1024 * 1024
LANES = 128
SUBLANES = 8
V7X_MXU_COLS = 256


def _params(*sem, vmem_limit_bytes=V7X_VMEM_LIMIT_BYTES):
    return pltpu.CompilerParams(dimension_semantics=sem, vmem_limit_bytes=vmem_limit_bytes)


def _rms(x, g):
    ms = jnp.mean(x * x, axis=-1, keepdims=True)
    return x * lax.rsqrt(ms + EPS) * g


def _split_bf16(x):
    hi = x.astype(BF16)
    lo = (x - hi.astype(F32)).astype(BF16)
    return hi, lo


def _ffn_body(x_hbm, g_ref, wg_ref, wu_ref, wd_ref, gn_ref, *refs, emit, bm):
    if emit == "x+norm":
        o_hbm, hn_hbm, acc_ref, h_ref, rs_ref, sem_x, sem_o, sem_h = refs
    else:
        o_hbm, acc_ref, h_ref, rs_ref, sem_x, sem_o = refs
    i, f = pl.program_id(0), pl.program_id(1)
    n_i, n_f = pl.num_programs(0), pl.num_programs(1)
    slot = lax.rem(i, 2)
    acc = acc_ref.at[slot]
    d = h_ref.shape[1]
    n_slabs = bm // NORM_ROWS

    def tile(t):
        return pl.ds(pl.multiple_of(t * bm, bm), bm)

    def x_copy(t, s):
        return pltpu.make_async_copy(x_hbm.at[tile(t)], acc_ref.at[s], sem_x.at[s])

    def y_copy(t, s):
        return pltpu.make_async_copy(acc_ref.at[s], o_hbm.at[tile(t)], sem_o.at[s])

    def norm_copy(t):
        return pltpu.make_async_copy(h_ref, hn_hbm.at[tile(t)], sem_h.at[0])

    def rows(s):
        return pl.ds(pl.multiple_of(s * NORM_ROWS, NORM_ROWS), NORM_ROWS)

    @pl.when(f == 0)
    def _():
        @pl.when(i == 0)
        def _():
            x_copy(0, 0).start()

        x_copy(i, slot).wait()
        if emit == "x+norm":
            @pl.when(i > 0)
            def _():
                norm_copy(i - 1).wait()

        g = g_ref[...]

        def slab(s, carry):
            h_ref[rows(s), :] = _rms(acc[rows(s), :], g).astype(BF16)
            return carry

        lax.fori_loop(0, n_slabs, slab, 0, unroll=NORM_UNROLL)

    @pl.when(f == 1)
    def _():
        @pl.when(i > 0)
        def _():
            y_copy(i - 1, 1 - slot).wait()

        @pl.when(i + 1 < n_i)
        def _():
            x_copy(i + 1, 1 - slot).start()

    h = h_ref[...]
    gate = jnp.dot(h, wg_ref[...].astype(BF16), preferred_element_type=F32)
    up = jnp.dot(h, wu_ref[...].astype(BF16), preferred_element_type=F32)
    act = (FFN_RES * (gate * jax.nn.sigmoid(gate) * up)).astype(BF16)
    for n0 in range(0, d, FFN_DOWN_COLS):
        sl = slice(n0, n0 + FFN_DOWN_COLS)
        acc[:, sl] += jnp.dot(act, wd_ref[:, sl].astype(BF16), preferred_element_type=F32)

    @pl.when(f == n_f - 1)
    def _():
        if emit != "x":
            gn = gn_ref[...]

            def stat_slab(s, carry):
                y = acc[rows(s), :]
                rs_ref[rows(s), :] = lax.rsqrt(jnp.mean(y * y, axis=-1, keepdims=True) + EPS)
                return carry

            lax.fori_loop(0, n_slabs, stat_slab, 0, unroll=NORM_UNROLL)

            def norm_slab(s, carry):
                normed = acc[rows(s), :] * rs_ref[rows(s), :] * gn
                if emit == "norm":
                    acc[rows(s), :] = normed
                else:
                    h_ref[rows(s), :] = normed.astype(BF16)
                return carry

            lax.fori_loop(0, n_slabs, norm_slab, 0, unroll=NORM_UNROLL)

        if emit == "x+norm":
            norm_copy(i).start()
        y_copy(i, slot).start()

        @pl.when(i == n_i - 1)
        def _():
            y_copy(i, slot).wait()
            if emit == "x+norm":
                norm_copy(i).wait()


def _ffn(x, g, wg, wu, wd, g_next, emit, *, bm=1024, bf=256):
    m, d = x.shape
    dff = wg.shape[1]
    assert m % bm == 0 and dff // bf >= 2
    vec = pl.BlockSpec((1, d), lambda i, f: (0, 0))
    hbm = pl.BlockSpec(memory_space=pl.ANY)
    out_specs, out_shape = hbm, jax.ShapeDtypeStruct((m, d), F32)
    scratch = [
        pltpu.VMEM((2, bm, d), F32),
        pltpu.VMEM((bm, d), BF16),
        pltpu.VMEM((bm, 1), F32),
        pltpu.SemaphoreType.DMA((2,)),
        pltpu.SemaphoreType.DMA((2,)),
    ]
    if emit == "x+norm":
        out_specs, out_shape = [hbm, hbm], [out_shape, jax.ShapeDtypeStruct((m, d), BF16)]
        scratch.append(pltpu.SemaphoreType.DMA((1,)))
    return pl.pallas_call(
        functools.partial(_ffn_body, emit=emit, bm=bm),
        grid=(m // bm, dff // bf),
        in_specs=[
            hbm,
            vec,
            pl.BlockSpec((d, bf), lambda i, f: (0, f)),
            pl.BlockSpec((d, bf), lambda i, f: (0, f)),
            pl.BlockSpec((bf, d), lambda i, f: (f, 0)),
            vec,
        ],
        out_specs=out_specs,
        out_shape=out_shape,
        scratch_shapes=scratch,
        compiler_params=_params("arbitrary", "arbitrary", vmem_limit_bytes=V7X_VMEM_LIMIT_LARGE_BYTES),
        name="ffn",
    )(x, g, wg, wu, wd, g_next)


class _SideCast:
    def __init__(self, src, block, grid):
        rows, cols = src.shape
        br, bc = block
        assert rows % br == 0 and cols % bc == 0 and (br == rows or bc == cols)
        n_blocks = (rows // br) * (cols // bc)
        assert grid[0] * grid[1] >= n_blocks
        n_inner = grid[1]

        def index(i, j):
            t = jnp.minimum(i * n_inner + j, n_blocks - 1)
            return (t, 0) if bc == cols else (0, t)

        self.src = src
        self.in_spec = pl.BlockSpec(block, index)
        self.out_spec = pl.BlockSpec(block, index)
        self.out_shape = jax.ShapeDtypeStruct(src.shape, BF16)

    @staticmethod
    def step(src_ref, dst_ref):
        dst_ref[...] = src_ref[...].astype(BF16)


def _mm_res_body(a_ref, w_ref, r_ref, o_ref):
    o_ref[...] = r_ref[...] + jnp.dot(a_ref[...], w_ref[...], preferred_element_type=F32)


def _matmul_res(a, w, res, *, bm=1024, bn=1024):
    m, k = a.shape
    n = w.shape[1]
    return pl.pallas_call(
        _mm_res_body,
        grid=(m // bm, n // bn),
        in_specs=[
            pl.BlockSpec((bm, k), lambda i, j: (i, 0)),
            pl.BlockSpec((k, bn), lambda i, j: (0, j)),
            pl.BlockSpec((bm, bn), lambda i, j: (i, j)),
        ],
        out_specs=pl.BlockSpec((bm, bn), lambda i, j: (i, j)),
        out_shape=jax.ShapeDtypeStruct((m, n), F32),
        compiler_params=_params("parallel", "arbitrary"),
        name="mix_out",
    )(a, w, res)


def _in_proj_body(a_ref, piece_ref, *refs, n_side):
    side_srcs, o_ref, side_dsts, wbf_ref = refs[:n_side], refs[n_side], refs[n_side + 1:-1], refs[-1]
    jj, i = pl.program_id(0), pl.program_id(1)
    for side_src, side_dst in zip(side_srcs, side_dsts):
        _SideCast.step(side_src, side_dst)

    piece = piece_ref.shape[0]

    def stage():
        wbf_ref[lax.rem(jj, 2), pl.ds(pl.multiple_of(i * piece, piece), piece), :] = piece_ref[...].astype(BF16)

    @pl.when(jj == 0)
    def _():
        stage()

    @pl.when(jj > 0)
    def _():
        stage()
        w = wbf_ref[lax.rem(jj + 1, 2)]
        nt = (((1,), (1,)), ((), ()))
        o_ref[...] = lax.dot_general(a_ref[...], w, nt, preferred_element_type=F32).astype(o_ref.dtype)


def _in_proj(a, wt, groups, sides, *, bm=1024, tile=1024):
    m, k = a.shape
    mi = m // bm
    piece = tile // mi
    assert piece % LANES == 0 and V7X_MXU_COLS % piece == 0
    (off0, rows0), (off1, rows1) = groups
    assert off0 % SUBLANES == 0 and off1 % SUBLANES == 0 and rows0 % tile == 0 and rows1 % tile == 0
    n0 = rows0 // tile
    n_tiles = n0 + rows1 // tile
    grid = (n_tiles + 1, mi)

    def piece_index(jj, i):
        t = jnp.minimum(jj, n_tiles - 1)
        p = jnp.where(jj < n_tiles, i, mi - 1)
        row = jnp.where(t < n0, off0 + t * tile, off1 + (t - n0) * tile) + p * piece
        return pl.multiple_of(row, SUBLANES), 0

    in_specs = [
        pl.BlockSpec((bm, k), lambda jj, i: (jnp.where(jj > 0, i, 0), 0)),
        pl.BlockSpec((pl.Element(piece), pl.Element(k)), piece_index),
    ]
    out_specs = [pl.BlockSpec((bm, tile), lambda jj, i: (jnp.where(jj > 0, i, 0), jnp.maximum(jj - 1, 0)))]
    out_shape = [jax.ShapeDtypeStruct((m, n_tiles * tile), BF16)]
    args = [a, wt]
    for src, block in sides:
        side = _SideCast(src, block, grid)
        in_specs.append(side.in_spec)
        args.append(side.src)
        out_specs.append(side.out_spec)
        out_shape.append(side.out_shape)
    return pl.pallas_call(
        functools.partial(_in_proj_body, n_side=len(sides)),
        grid=grid,
        in_specs=in_specs,
        out_specs=out_specs,
        out_shape=out_shape,
        scratch_shapes=[pltpu.VMEM((2, tile, k), BF16)],
        compiler_params=_params("arbitrary", "arbitrary", vmem_limit_bytes=V7X_VMEM_LIMIT_LARGE_BYTES),
        name="in_proj",
    )(*args)


def _conv_body(cb_ref, cc_ref, cu_ref, w_ref, b_ref, o_ref):
    p = cc_ref[0].astype(F32) * cu_ref[0].astype(F32)
    row = lax.broadcasted_iota(jnp.int32, p.shape, 0)
    p1 = jnp.where(row >= 1, pltpu.roll(p, 1, axis=0), 0.0)
    p2 = jnp.where(row >= 2, pltpu.roll(p, 2, axis=0), 0.0)
    w = w_ref[...]
    conv = b_ref[...] + w[0:1, :] * p2
    conv = conv + w[1:2, :] * p1
    conv = conv + w[2:3, :] * p
    o_ref[0] = (cb_ref[0].astype(F32) * conv).astype(o_ref.dtype)


def _conv_branch(proj3, conv_w, conv_b, d_conv, *, bc=256):
    b, s, _ = proj3.shape
    nb = d_conv // bc
    return pl.pallas_call(
        _conv_body,
        grid=(b, nb),
        in_specs=[
            pl.BlockSpec((1, s, bc), lambda i, j: (i, 0, j)),
            pl.BlockSpec((1, s, bc), lambda i, j: (i, 0, nb + j)),
            pl.BlockSpec((1, s, bc), lambda i, j: (i, 0, 2 * nb + j)),
            pl.BlockSpec((CONV_W, bc), lambda i, j: (0, j)),
            pl.BlockSpec((1, bc), lambda i, j: (0, j)),
        ],
        out_specs=pl.BlockSpec((1, s, bc), lambda i, j: (i, 0, j)),
        out_shape=jax.ShapeDtypeStruct((b, s, d_conv), BF16),
        compiler_params=_params("parallel", "parallel"),
        name="conv",
    )(proj3, proj3, proj3, conv_w, conv_b)


def _loggate_body(h_ref, wl_ref, wup_ref, ba_ref, o_ref):
    nt = (((1,), (1,)), ((), ()))
    a = lax.dot_general(h_ref[...], wl_ref[...].astype(BF16), nt, preferred_element_type=F32)
    a_hi, a_lo = _split_bf16(a)
    w_hi, w_lo = _split_bf16(wup_ref[...])
    z = (jnp.dot(a_hi, w_hi, preferred_element_type=F32) + jnp.dot(a_hi, w_lo, preferred_element_type=F32)
         + jnp.dot(a_lo, w_hi, preferred_element_type=F32)) + ba_ref[...]
    o_ref[...] = (jnp.minimum(z, 0.0) - jnp.log(1.0 + jnp.exp(-jnp.abs(z)))) * (1.0 / GATE_TEMP)


def _loggate(h, wt, row_off, w_up, b_alpha, *, bm=1024):
    m, d = h.shape
    rank, dk = w_up.shape
    return pl.pallas_call(
        _loggate_body,
        grid=(m // bm,),
        in_specs=[
            pl.BlockSpec((bm, d), lambda i: (i, 0)),
            pl.BlockSpec((pl.Element(rank), pl.Element(d)), lambda i: (row_off, 0)),
            pl.BlockSpec((rank, dk), lambda i: (0, 0)),
            pl.BlockSpec((1, dk), lambda i: (0, 0)),
        ],
        out_specs=pl.BlockSpec((bm, dk), lambda i: (i, 0)),
        out_shape=jax.ShapeDtypeStruct((m, dk), F32),
        compiler_params=_params("parallel"),
        name="loggate",
    )(h, wt, w_up, b_alpha)


def _gla_tables():
    c = CHUNK
    i = np.arange(c)[:, None]
    t = np.arange(c)[None, :]
    blocks = [(t <= i), (t > i)]
    masks = []
    for lvl in range(N_LEVELS):
        p = c >> lvl
        half = p // 2
        mid = (i // p) * p + half
        upper = i >= mid
        blocks.append(np.where(upper, (t > mid) & (t <= i), (t > i) & (t <= mid)))
        j = t
        same_parent = (i // p) == (j // p)
        masks.append(same_parent & (((i % p) >= half) != ((j % p) >= half)))
    expo = np.concatenate(blocks, axis=0).astype(np.float32)
    mask = np.stack(masks).astype(np.float32)
    return expo, mask


def _gla_body(q_ref, k_ref, v_ref, r_ref, la_ref, gn_ref, expo_ref, mask_ref, *refs, scale, hk, hv, n_side):
    c = CHUNK
    side_srcs, o_ref, side_dsts, state_ref = refs[:n_side], refs[n_side], refs[n_side + 1:-1], refs[-1]
    for side_src, side_dst in zip(side_srcs, side_dsts):
        _SideCast.step(side_src, side_dst)
    nt = (((1,), (1,)), ((), ()))
    tn = (((0,), (0,)), ((), ()))

    @pl.when(pl.program_id(1) == 0)
    def _():
        state_ref[...] = jnp.zeros_like(state_ref)

    row = lax.broadcasted_iota(jnp.int32, (c, c), 0)
    col = lax.broadcasted_iota(jnp.int32, (c, c), 1)
    eye = row == col
    for h in range(GLA_HEADS):
        ks = slice(h * hk, (h + 1) * hk)
        vs = slice(h * hv, (h + 1) * hv)
        state = state_ref[h]
        for cc in range(GLA_CHUNKS_PER_STEP):
            rs = slice(cc * c, (cc + 1) * c)
            q = q_ref[0, rs, ks].astype(F32) * scale
            k = k_ref[0, rs, ks].astype(F32)
            v = v_ref[0, rs, vs]

            hi, lo = _split_bf16(la_ref[0, rs, ks])
            expo = jnp.dot(expo_ref[...], jnp.concatenate([hi, lo], axis=0), preferred_element_type=F32)
            e_all = jnp.exp(expo)
            e_l = e_all[0:c]
            e_rest = e_all[c:2 * c]

            scores = jnp.where(eye, jnp.sum(q * k, axis=-1, keepdims=True), 0.0)
            for lvl in range(N_LEVELS):
                e = e_all[(2 + lvl) * c:(3 + lvl) * c]
                p = lax.dot_general((q * e).astype(BF16), (k * e).astype(BF16), nt, preferred_element_type=F32)
                scores = scores + mask_ref[lvl] * p

            o = jnp.dot(scores.astype(BF16), v, preferred_element_type=F32)
            o = o + lax.dot_general((q * e_l).astype(BF16), state.astype(BF16), nt, preferred_element_type=F32)

            k_dec = (k * e_rest).astype(BF16)
            state = state * e_l[c - 1:c, :] + lax.dot_general(v, k_dec, tn, preferred_element_type=F32)

            r = r_ref[0, rs, vs].astype(F32)
            o_ref[0, rs, vs] = (_rms(o, gn_ref[...]) * (r * jax.nn.sigmoid(r))).astype(o_ref.dtype)
        state_ref[h] = state


def _gla(proj3, la3, gn, sides, *, q_off, k_off, v_off, r_off, hk, hv):
    b, s, _ = proj3.shape
    rows = CHUNK * GLA_CHUNKS_PER_STEP
    nc = s // rows
    dk, dv = GLA_HEADS * hk, GLA_HEADS * hv
    expo, mask = _gla_tables()
    expo2 = np.concatenate([expo, expo], axis=1)
    body = functools.partial(_gla_body, scale=float(hk) ** -0.5, hk=hk, hv=hv, n_side=len(sides))
    sides = [_SideCast(src, block, (b, nc)) for src, block in sides]
    return pl.pallas_call(
        body,
        grid=(b, nc),
        in_specs=[
            pl.BlockSpec((1, rows, dk), lambda i, c: (i, c, q_off // dk)),
            pl.BlockSpec((1, rows, dk), lambda i, c: (i, c, k_off // dk)),
            pl.BlockSpec((1, rows, dv), lambda i, c: (i, c, v_off // dv)),
            pl.BlockSpec((1, rows, dv), lambda i, c: (i, c, r_off // dv)),
            pl.BlockSpec((1, rows, dk), lambda i, c: (i, c, 0)),
            pl.BlockSpec((1, hv), lambda i, c: (0, 0)),
            pl.BlockSpec(expo2.shape, lambda i, c: (0, 0)),
            pl.BlockSpec(mask.shape, lambda i, c: (0, 0, 0)),
            *[side.in_spec for side in sides],
        ],
        out_specs=[pl.BlockSpec((1, rows, dv), lambda i, c: (i, c, 0)), *[side.out_spec for side in sides]],
        out_shape=[jax.ShapeDtypeStruct((b, s, dv), BF16), *[side.out_shape for side in sides]],
        scratch_shapes=[pltpu.VMEM((GLA_HEADS, hv, hk), F32)],
        compiler_params=_params("arbitrary", "arbitrary"),
        name="gla",
    )(proj3, proj3, proj3, proj3, la3, gn, jnp.asarray(expo2, dtype=BF16), jnp.asarray(mask),
      *[side.src for side in sides])


def _merge_body(ua_ref, ub_ref, wa_ref, wb_ref, ga_ref, gb_ref, bm_ref, o_ref):
    ya = jnp.dot(ua_ref[...], wa_ref[...], preferred_element_type=F32)
    yb = jnp.dot(ub_ref[...], wb_ref[...], preferred_element_type=F32)
    bias = bm_ref[...]
    sa = jax.nn.sigmoid(ga_ref[...].astype(F32) + bias[0:1, :])
    sb = jax.nn.sigmoid(gb_ref[...].astype(F32) + bias[1:2, :])
    o_ref[...] = (sa * ya + sb * yb).astype(o_ref.dtype)


def _merge(ua, ub, wa, wb, proj, gate_col, b_merge, *, bm=1024, bn=1024):
    m, kk = ua.shape
    n = wa.shape[1]
    nb = n // bn
    g0 = gate_col // bn
    assert gate_col % bn == 0
    return pl.pallas_call(
        _merge_body,
        grid=(m // bm, nb),
        in_specs=[
            pl.BlockSpec((bm, kk), lambda i, j: (i, 0)),
            pl.BlockSpec((bm, kk), lambda i, j: (i, 0)),
            pl.BlockSpec((kk, bn), lambda i, j: (0, j)),
            pl.BlockSpec((kk, bn), lambda i, j: (0, j)),
            pl.BlockSpec((bm, bn), lambda i, j: (i, g0 + j)),
            pl.BlockSpec((bm, bn), lambda i, j: (i, g0 + nb + j)),
            pl.BlockSpec((2, bn), lambda i, j: (0, j)),
        ],
        out_specs=pl.BlockSpec((bm, bn), lambda i, j: (i, j)),
        out_shape=jax.ShapeDtypeStruct((m, n), BF16),
        compiler_params=_params("parallel", "arbitrary"),
        name="merge",
    )(ua, ub, wa, wb, proj, proj, b_merge)


def kernel(x, ffn1_norm_g, ffn1_w_gate, ffn1_w_up, ffn1_w_down, mix_norm_g, w_in, conv_w, conv_b, w_conv_out,
           w_alpha_up, b_alpha, gla_norm_g, w_gla_out, b_merge, w_mix_out, ffn2_norm_g, ffn2_w_gate, ffn2_w_up,
           ffn2_w_down, final_norm_g):
    batch, seq, d_model = x.shape
    depth = ffn1_w_gate.shape[0]
    d_conv = conv_w.shape[-1]
    rank, d_gla_k = w_alpha_up.shape[-2:]
    d_gla_v = w_gla_out.shape[-2]
    hk, hv = d_gla_k // GLA_HEADS, d_gla_v // GLA_HEADS
    q_off = 3 * d_conv
    k_off = q_off + d_gla_k
    v_off = k_off + d_gla_k
    r_off = v_off + d_gla_v
    a_off = r_off + d_gla_v
    g_off = a_off + rank
    m = batch * seq

    xs = x.reshape(m, d_model)
    for l in range(depth):
        xs, h = _ffn(xs, ffn1_norm_g[l][None], ffn1_w_gate[l].astype(BF16), ffn1_w_up[l], ffn1_w_down[l],
                     mix_norm_g[l][None], "x+norm")

        wt = jnp.transpose(w_in[l])
        proj, w2_gate, w2_up = _in_proj(
            h, wt, [(0, a_off), (g_off, 2 * d_model)],
            [(ffn2_w_gate[l], (d_model, IN_PROJ_SIDE_COLS)), (ffn2_w_up[l], (d_model, IN_PROJ_SIDE_COLS))])
        la = _loggate(h, wt, a_off, w_alpha_up[l], b_alpha[l][None])

        proj3 = proj.reshape(batch, seq, proj.shape[1])
        ua = _conv_branch(proj3, conv_w[l], conv_b[l][None], d_conv)
        gla_steps = batch * (seq // (CHUNK * GLA_CHUNKS_PER_STEP))
        ub, w2_down, w_mix, w_conv, w_gla = _gla(
            proj3, la.reshape(batch, seq, d_gla_k), gla_norm_g[l][None],
            [(ffn2_w_down[l], (GLA_SIDE_ROWS, d_model)),
             (w_mix_out[l], (d_model // gla_steps, d_model)),
             (w_conv_out[l], (d_conv // gla_steps, d_model)),
             (w_gla_out[l], (d_gla_v // gla_steps, d_model))],
            q_off=q_off, k_off=k_off, v_off=v_off, r_off=r_off, hk=hk, hv=hv)
        merged = _merge(ua.reshape(m, d_conv), ub.reshape(m, d_gla_v), w_conv, w_gla, proj, a_off, b_merge[l])
        xs = _matmul_res(merged, w_mix, xs)

        emit = "norm" if l == depth - 1 else "x"
        xs = _ffn(xs, ffn2_norm_g[l][None], w2_gate, w2_up, w2_down, final_norm_g[None], emit)
    return xs.reshape(batch, seq, d_model)
```

```python
import functools

import jax
import jax.numpy as jnp
import numpy as np
from jax import lax
from jax.experimental import pallas as pl
from jax.experimental.pallas import tpu as pltpu

F32 = jnp.float32
BF16 = jnp.bfloat16

EPS = 1e-6
FFN_RES = 0.5
CHUNK = 64
CONV_W = 3
GLA_HEADS = 4
GATE_TEMP = 16.0
N_LEVELS = 6
GLA_CHUNKS_PER_STEP = 2
FFN_DOWN_COLS = 512
NORM_ROWS = 16
NORM_UNROLL = 8
IN_PROJ_SIDE_COLS = 128
GLA_SIDE_ROWS = 256

V7X_VMEM_LIMIT_BYTES = 58 * 1024 * 1024
V7X_VMEM_LIMIT_LARGE_BYTES = 62 * 1024 * 1024
LANES = 128
SUBLANES = 8
V7X_MXU_COLS = 256


def _params(*sem, vmem_limit_bytes=V7X_VMEM_LIMIT_BYTES):
    return pltpu.CompilerParams(dimension_semantics=sem, vmem_limit_bytes=vmem_limit_bytes)


def _rms(x, g):
    ms = jnp.mean(x * x, axis=-1, keepdims=True)
    return x * lax.rsqrt(ms + EPS) * g


def _split_bf16(x):
    hi = x.astype(BF16)
    lo = (x - hi.astype(F32)).astype(BF16)
    return hi, lo


def _ffn_body(x_hbm, g_ref, wg_ref, wu_ref, wd_ref, gn_ref, *refs, emit, bm):
    if emit == "x+norm":
        o_hbm, hn_hbm, acc_ref, h_ref, rs_ref, sem_x, sem_o, sem_h = refs
    else:
        o_hbm, acc_ref, h_ref, rs_ref, sem_x, sem_o = refs
    i, f = pl.program_id(0), pl.program_id(1)
    n_i, n_f = pl.num_programs(0), pl.num_programs(1)
    slot = lax.rem(i, 2)
    acc = acc_ref.at[slot]
    d = h_ref.shape[1]
    n_slabs = bm // NORM_ROWS

    def tile(t):
        return pl.ds(pl.multiple_of(t * bm, bm), bm)

    def x_copy(t, s):
        return pltpu.make_async_copy(x_hbm.at[tile(t)], acc_ref.at[s], sem_x.at[s])

    def y_copy(t, s):
        return pltpu.make_async_copy(acc_ref.at[s], o_hbm.at[tile(t)], sem_o.at[s])

    def norm_copy(t, half):
        hb = bm // 2
        dst_rows = pl.ds(pl.multiple_of(t * bm + half * hb, hb), hb)
        return pltpu.make_async_copy(h_ref.at[pl.ds(half * hb, hb)], hn_hbm.at[dst_rows], sem_h.at[half])

    def rows(s):
        return pl.ds(pl.multiple_of(s * NORM_ROWS, NORM_ROWS), NORM_ROWS)

    @pl.when(f == 0)
    def _():
        @pl.when(i == 0)
        def _():
            x_copy(0, 0).start()

        x_copy(i, slot).wait()
        if emit == "x+norm":
            @pl.when(i > 0)
            def _():
                norm_copy(i - 1, 0).wait()
                norm_copy(i - 1, 1).wait()
                y_copy(i - 1, 1 - slot).start()

        g = g_ref[...]

        def slab(s, carry):
            h_ref[rows(s), :] = _rms(acc[rows(s), :], g).astype(BF16)
            return carry

        lax.fori_loop(0, n_slabs, slab, 0, unroll=NORM_UNROLL)

    @pl.when(f == 1)
    def _():
        @pl.when(i > 0)
        def _():
            y_copy(i - 1, 1 - slot).wait()

        @pl.when(i + 1 < n_i)
        def _():
            x_copy(i + 1, 1 - slot).start()

    h = h_ref[...]
    gate = jnp.dot(h, wg_ref[...].astype(BF16), preferred_element_type=F32)
    up = jnp.dot(h, wu_ref[...].astype(BF16), preferred_element_type=F32)
    act = (FFN_RES * (gate * jax.nn.sigmoid(gate) * up)).astype(BF16)
    for n0 in range(0, d, FFN_DOWN_COLS):
        sl = slice(n0, n0 + FFN_DOWN_COLS)
        acc[:, sl] += jnp.dot(act, wd_ref[:, sl].astype(BF16), preferred_element_type=F32)

    @pl.when(f == n_f - 1)
    def _():
        if emit != "x":
            gn = gn_ref[...]

            def stat_slab(s, carry):
                y = acc[rows(s), :]
                rs_ref[rows(s), :] = lax.rsqrt(jnp.mean(y * y, axis=-1, keepdims=True) + EPS)
                return carry

            lax.fori_loop(0, n_slabs, stat_slab, 0, unroll=NORM_UNROLL)

            def norm_slab(s, carry):
                normed = acc[rows(s), :] * rs_ref[rows(s), :] * gn
                if emit == "norm":
                    acc[rows(s), :] = normed
                else:
                    h_ref[rows(s), :] = normed.astype(BF16)
                return carry

            if emit == "norm":
                lax.fori_loop(0, n_slabs, norm_slab, 0, unroll=NORM_UNROLL)
            else:
                lax.fori_loop(0, n_slabs // 2, norm_slab, 0, unroll=NORM_UNROLL)
                norm_copy(i, 0).start()
                lax.fori_loop(n_slabs // 2, n_slabs, norm_slab, 0, unroll=NORM_UNROLL)
                norm_copy(i, 1).start()

        last_tile = i == n_i - 1
        if emit == "x+norm":
            @pl.when(last_tile)
            def _():
                y_copy(i, slot).start()
        else:
            y_copy(i, slot).start()

        @pl.when(last_tile)
        def _():
            y_copy(i, slot).wait()
            if emit == "x+norm":
                norm_copy(i, 0).wait()
                norm_copy(i, 1).wait()


def _ffn(x, g, wg, wu, wd, g_next, emit, *, bm=1024, bf=256):
    m, d = x.shape
    dff = wg.shape[1]
    assert m % bm == 0 and dff // bf >= 2
    vec = pl.BlockSpec((1, d), lambda i, f: (0, 0))
    hbm = pl.BlockSpec(memory_space=pl.ANY)
    out_specs, out_shape = hbm, jax.ShapeDtypeStruct((m, d), F32)
    scratch = [
        pltpu.VMEM((2, bm, d), F32),
        pltpu.VMEM((bm, d), BF16),
        pltpu.VMEM((bm, 1), F32),
        pltpu.SemaphoreType.DMA((2,)),
        pltpu.SemaphoreType.DMA((2,)),
    ]
    if emit == "x+norm":
        out_specs, out_shape = [hbm, hbm], [out_shape, jax.ShapeDtypeStruct((m, d), BF16)]
        scratch.append(pltpu.SemaphoreType.DMA((2,)))
    return pl.pallas_call(
        functools.partial(_ffn_body, emit=emit, bm=bm),
        grid=(m // bm, dff // bf),
        in_specs=[
            hbm,
            vec,
            pl.BlockSpec((d, bf), lambda i, f: (0, f)),
            pl.BlockSpec((d, bf), lambda i, f: (0, f)),
            pl.BlockSpec((bf, d), lambda i, f: (f, 0)),
            vec,
        ],
        out_specs=out_specs,
        out_shape=out_shape,
        scratch_shapes=scratch,
        compiler_params=_params("arbitrary", "arbitrary", vmem_limit_bytes=V7X_VMEM_LIMIT_LARGE_BYTES),
        name="ffn",
    )(x, g, wg, wu, wd, g_next)


class _SideCast:
    def __init__(self, src, block, grid):
        rows, cols = src.shape
        br, bc = block
        assert rows % br == 0 and cols % bc == 0 and (br == rows or bc == cols)
        n_blocks = (rows // br) * (cols // bc)
        assert grid[0] * grid[1] >= n_blocks
        n_inner = grid[1]

        def index(i, j):
            t = jnp.minimum(i * n_inner + j, n_blocks - 1)
            return (t, 0) if bc == cols else (0, t)

        self.src = src
        self.in_spec = pl.BlockSpec(block, index)
        self.out_spec = pl.BlockSpec(block, index)
        self.out_shape = jax.ShapeDtypeStruct(src.shape, BF16)

    @staticmethod
    def step(src_ref, dst_ref):
        dst_ref[...] = src_ref[...].astype(BF16)


def _mm_res_body(a_ref, w_ref, r_ref, o_ref):
    o_ref[...] = r_ref[...] + jnp.dot(a_ref[...], w_ref[...], preferred_element_type=F32)


def _matmul_res(a, w, res, *, bm=1024, bn=1024):
    m, k = a.shape
    n = w.shape[1]
    return pl.pallas_call(
        _mm_res_body,
        grid=(m // bm, n // bn),
        in_specs=[
            pl.BlockSpec((bm, k), lambda i, j: (i, 0)),
            pl.BlockSpec((k, bn), lambda i, j: (0, j)),
            pl.BlockSpec((bm, bn), lambda i, j: (i, j)),
        ],
        out_specs=pl.BlockSpec((bm, bn), lambda i, j: (i, j)),
        out_shape=jax.ShapeDtypeStruct((m, n), F32),
        compiler_params=_params("parallel", "arbitrary"),
        name="mix_out",
    )(a, w, res)


def _in_proj_body(a_ref, piece_ref, *refs, n_side):
    side_srcs, o_ref, side_dsts, wbf_ref = refs[:n_side], refs[n_side], refs[n_side + 1:-1], refs[-1]
    jj, i = pl.program_id(0), pl.program_id(1)
    for side_src, side_dst in zip(side_srcs, side_dsts):
        _SideCast.step(side_src, side_dst)

    piece = piece_ref.shape[0]

    def stage():
        wbf_ref[lax.rem(jj, 2), pl.ds(pl.multiple_of(i * piece, piece), piece), :] = piece_ref[...].astype(BF16)

    @pl.when(jj == 0)
    def _():
        stage()

    @pl.when(jj > 0)
    def _():
        stage()
        w = wbf_ref[lax.rem(jj + 1, 2)]
        nt = (((1,), (1,)), ((), ()))
        o_ref[...] = lax.dot_general(a_ref[...], w, nt, preferred_element_type=F32).astype(o_ref.dtype)


def _in_proj(a, wt, groups, sides, *, bm=1024, tile=1024):
    m, k = a.shape
    mi = m // bm
    piece = tile // mi
    assert piece % LANES == 0 and V7X_MXU_COLS % piece == 0
    (off0, rows0), (off1, rows1) = groups
    assert off0 % SUBLANES == 0 and off1 % SUBLANES == 0 and rows0 % tile == 0 and rows1 % tile == 0
    n0 = rows0 // tile
    n_tiles = n0 + rows1 // tile
    grid = (n_tiles + 1, mi)

    def piece_index(jj, i):
        t = jnp.minimum(jj, n_tiles - 1)
        p = jnp.where(jj < n_tiles, i, mi - 1)
        row = jnp.where(t < n0, off0 + t * tile, off1 + (t - n0) * tile) + p * piece
        return pl.multiple_of(row, SUBLANES), 0

    in_specs = [
        pl.BlockSpec((bm, k), lambda jj, i: (jnp.where(jj > 0, i, 0), 0)),
        pl.BlockSpec((pl.Element(piece), pl.Element(k)), piece_index),
    ]
    out_specs = [pl.BlockSpec((bm, tile), lambda jj, i: (jnp.where(jj > 0, i, 0), jnp.maximum(jj - 1, 0)))]
    out_shape = [jax.ShapeDtypeStruct((m, n_tiles * tile), BF16)]
    args = [a, wt]
    for src, block in sides:
        side = _SideCast(src, block, grid)
        in_specs.append(side.in_spec)
        args.append(side.src)
        out_specs.append(side.out_spec)
        out_shape.append(side.out_shape)
    return pl.pallas_call(
        functools.partial(_in_proj_body, n_side=len(sides)),
        grid=grid,
        in_specs=in_specs,
        out_specs=out_specs,
        out_shape=out_shape,
        scratch_shapes=[pltpu.VMEM((2, tile, k), BF16)],
        compiler_params=_params("arbitrary", "arbitrary", vmem_limit_bytes=V7X_VMEM_LIMIT_LARGE_BYTES),
        name="in_proj",
    )(*args)


def _conv_body(cb_ref, cc_ref, cu_ref, w_ref, b_ref, o_ref):
    p = cc_ref[0].astype(F32) * cu_ref[0].astype(F32)
    row = lax.broadcasted_iota(jnp.int32, p.shape, 0)
    p1 = jnp.where(row >= 1, pltpu.roll(p, 1, axis=0), 0.0)
    p2 = jnp.where(row >= 2, pltpu.roll(p, 2, axis=0), 0.0)
    w = w_ref[...]
    conv = b_ref[...] + w[0:1, :] * p2
    conv = conv + w[1:2, :] * p1
    conv = conv + w[2:3, :] * p
    o_ref[0] = (cb_ref[0].astype(F32) * conv).astype(o_ref.dtype)


def _conv_branch(proj3, conv_w, conv_b, d_conv, *, bc=256):
    b, s, _ = proj3.shape
    nb = d_conv // bc
    return pl.pallas_call(
        _conv_body,
        grid=(b, nb),
        in_specs=[
            pl.BlockSpec((1, s, bc), lambda i, j: (i, 0, j)),
            pl.BlockSpec((1, s, bc), lambda i, j: (i, 0, nb + j)),
            pl.BlockSpec((1, s, bc), lambda i, j: (i, 0, 2 * nb + j)),
            pl.BlockSpec((CONV_W, bc), lambda i, j: (0, j)),
            pl.BlockSpec((1, bc), lambda i, j: (0, j)),
        ],
        out_specs=pl.BlockSpec((1, s, bc), lambda i, j: (i, 0, j)),
        out_shape=jax.ShapeDtypeStruct((b, s, d_conv), BF16),
        compiler_params=_params("parallel", "parallel"),
        name="conv",
    )(proj3, proj3, proj3, conv_w, conv_b)


def _loggate_body(h_ref, wl_ref, wup_ref, ba_ref, o_ref):
    nt = (((1,), (1,)), ((), ()))
    a = lax.dot_general(h_ref[...], wl_ref[...].astype(BF16), nt, preferred_element_type=F32)
    a_hi, a_lo = _split_bf16(a)
    w_hi, w_lo = _split_bf16(wup_ref[...])
    z = (jnp.dot(a_hi, w_hi, preferred_element_type=F32) + jnp.dot(a_hi, w_lo, preferred_element_type=F32)
         + jnp.dot(a_lo, w_hi, preferred_element_type=F32)) + ba_ref[...]
    o_ref[...] = (jnp.minimum(z, 0.0) - jnp.log(1.0 + jnp.exp(-jnp.abs(z)))) * (1.0 / GATE_TEMP)


def _loggate(h, wt, row_off, w_up, b_alpha, *, bm=1024):
    m, d = h.shape
    rank, dk = w_up.shape
    return pl.pallas_call(
        _loggate_body,
        grid=(m // bm,),
        in_specs=[
            pl.BlockSpec((bm, d), lambda i: (i, 0)),
            pl.BlockSpec((pl.Element(rank), pl.Element(d)), lambda i: (row_off, 0)),
            pl.BlockSpec((rank, dk), lambda i: (0, 0)),
            pl.BlockSpec((1, dk), lambda i: (0, 0)),
        ],
        out_specs=pl.BlockSpec((bm, dk), lambda i: (i, 0)),
        out_shape=jax.ShapeDtypeStruct((m, dk), F32),
        compiler_params=_params("parallel"),
        name="loggate",
    )(h, wt, w_up, b_alpha)


def _gla_tables():
    c = CHUNK
    i = np.arange(c)[:, None]
    t = np.arange(c)[None, :]
    blocks = [(t <= i), (t > i)]
    masks = []
    for lvl in range(N_LEVELS):
        p = c >> lvl
        half = p // 2
        mid = (i // p) * p + half
        upper = i >= mid
        blocks.append(np.where(upper, (t > mid) & (t <= i), (t > i) & (t <= mid)))
        j = t
        same_parent = (i // p) == (j // p)
        masks.append(same_parent & (((i % p) >= half) != ((j % p) >= half)))
    expo = np.concatenate(blocks, axis=0).astype(np.float32)
    mask = np.stack(masks).astype(np.float32)
    return expo, mask


def _gla_body(q_ref, k_ref, v_ref, r_ref, la_ref, gn_ref, expo_ref, mask_ref, *refs, scale, hk, hv, n_side):
    c = CHUNK
    side_srcs, o_ref, side_dsts, state_ref = refs[:n_side], refs[n_side], refs[n_side + 1:-1], refs[-1]
    for side_src, side_dst in zip(side_srcs, side_dsts):
        _SideCast.step(side_src, side_dst)
    nt = (((1,), (1,)), ((), ()))
    tn = (((0,), (0,)), ((), ()))

    @pl.when(pl.program_id(1) == 0)
    def _():
        state_ref[...] = jnp.zeros_like(state_ref)

    row = lax.broadcasted_iota(jnp.int32, (c, c), 0)
    col = lax.broadcasted_iota(jnp.int32, (c, c), 1)
    eye = row == col
    for h in range(GLA_HEADS):
        ks = slice(h * hk, (h + 1) * hk)
        vs = slice(h * hv, (h + 1) * hv)
        state = state_ref[h]
        for cc in range(GLA_CHUNKS_PER_STEP):
            rs = slice(cc * c, (cc + 1) * c)
            q = q_ref[0, rs, ks].astype(F32) * scale
            k = k_ref[0, rs, ks].astype(F32)
            v = v_ref[0, rs, vs]

            hi, lo = _split_bf16(la_ref[0, rs, ks])
            expo = jnp.dot(expo_ref[...], jnp.concatenate([hi, lo], axis=0), preferred_element_type=F32)
            e_all = jnp.exp(expo)
            e_l = e_all[0:c]
            e_rest = e_all[c:2 * c]

            scores = jnp.where(eye, jnp.sum(q * k, axis=-1, keepdims=True), 0.0)
            for lvl in range(N_LEVELS):
                e = e_all[(2 + lvl) * c:(3 + lvl) * c]
                p = lax.dot_general((q * e).astype(BF16), (k * e).astype(BF16), nt, preferred_element_type=F32)
                scores = scores + mask_ref[lvl] * p

            o = jnp.dot(scores.astype(BF16), v, preferred_element_type=F32)
            o = o + lax.dot_general((q * e_l).astype(BF16), state.astype(BF16), nt, preferred_element_type=F32)

            k_dec = (k * e_rest).astype(BF16)
            state = state * e_l[c - 1:c, :] + lax.dot_general(v, k_dec, tn, preferred_element_type=F32)

            r = r_ref[0, rs, vs].astype(F32)
            o_ref[0, rs, vs] = (_rms(o, gn_ref[...]) * (r * jax.nn.sigmoid(r))).astype(o_ref.dtype)
        state_ref[h] = state


def _gla(proj3, la3, gn, sides, *, q_off, k_off, v_off, r_off, hk, hv):
    b, s, _ = proj3.shape
    rows = CHUNK * GLA_CHUNKS_PER_STEP
    nc = s // rows
    dk, dv = GLA_HEADS * hk, GLA_HEADS * hv
    expo, mask = _gla_tables()
    expo2 = np.concatenate([expo, expo], axis=1)
    body = functools.partial(_gla_body, scale=float(hk) ** -0.5, hk=hk, hv=hv, n_side=len(sides))
    sides = [_SideCast(src, block, (b, nc)) for src, block in sides]
    return pl.pallas_call(
        body,
        grid=(b, nc),
        in_specs=[
            pl.BlockSpec((1, rows, dk), lambda i, c: (i, c, q_off // dk)),
            pl.BlockSpec((1, rows, dk), lambda i, c: (i, c, k_off // dk)),
            pl.BlockSpec((1, rows, dv), lambda i, c: (i, c, v_off // dv)),
            pl.BlockSpec((1, rows, dv), lambda i, c: (i, c, r_off // dv)),
            pl.BlockSpec((1, rows, dk), lambda i, c: (i, c, 0)),
            pl.BlockSpec((1, hv), lambda i, c: (0, 0)),
            pl.BlockSpec(expo2.shape, lambda i, c: (0, 0)),
            pl.BlockSpec(mask.shape, lambda i, c: (0, 0, 0)),
            *[side.in_spec for side in sides],
        ],
        out_specs=[pl.BlockSpec((1, rows, dv), lambda i, c: (i, c, 0)), *[side.out_spec for side in sides]],
        out_shape=[jax.ShapeDtypeStruct((b, s, dv), BF16), *[side.out_shape for side in sides]],
        scratch_shapes=[pltpu.VMEM((GLA_HEADS, hv, hk), F32)],
        compiler_params=_params("arbitrary", "arbitrary"),
        name="gla",
    )(proj3, proj3, proj3, proj3, la3, gn, jnp.asarray(expo2, dtype=BF16), jnp.asarray(mask),
      *[side.src for side in sides])


def _merge_body(ua_ref, ub_ref, wa_ref, wb_ref, ga_ref, gb_ref, bm_ref, o_ref):
    ya = jnp.dot(ua_ref[...], wa_ref[...], preferred_element_type=F32)
    yb = jnp.dot(ub_ref[...], wb_ref[...], preferred_element_type=F32)
    bias = bm_ref[...]
    sa = jax.nn.sigmoid(ga_ref[...].astype(F32) + bias[0:1, :])
    sb = jax.nn.sigmoid(gb_ref[...].astype(F32) + bias[1:2, :])
    o_ref[...] = (sa * ya + sb * yb).astype(o_ref.dtype)


def _merge(ua, ub, wa, wb, proj, gate_col, b_merge, *, bm=1024, bn=1024):
    m, kk = ua.shape
    n = wa.shape[1]
    nb = n // bn
    g0 = gate_col // bn
    assert gate_col % bn == 0
    return pl.pallas_call(
        _merge_body,
        grid=(m // bm, nb),
        in_specs=[
            pl.BlockSpec((bm, kk), lambda i, j: (i, 0)),
            pl.BlockSpec((bm, kk), lambda i, j: (i, 0)),
            pl.BlockSpec((kk, bn), lambda i, j: (0, j)),
            pl.BlockSpec((kk, bn), lambda i, j: (0, j)),
            pl.BlockSpec((bm, bn), lambda i, j: (i, g0 + j)),
            pl.BlockSpec((bm, bn), lambda i, j: (i, g0 + nb + j)),
            pl.BlockSpec((2, bn), lambda i, j: (0, j)),
        ],
        out_specs=pl.BlockSpec((bm, bn), lambda i, j: (i, j)),
        out_shape=jax.ShapeDtypeStruct((m, n), BF16),
        compiler_params=_params("parallel", "arbitrary"),
        name="merge",
    )(ua, ub, wa, wb, proj, proj, b_merge)


def kernel(x, ffn1_norm_g, ffn1_w_gate, ffn1_w_up, ffn1_w_down, mix_norm_g, w_in, conv_w, conv_b, w_conv_out,
           w_alpha_up, b_alpha, gla_norm_g, w_gla_out, b_merge, w_mix_out, ffn2_norm_g, ffn2_w_gate, ffn2_w_up,
           ffn2_w_down, final_norm_g):
    batch, seq, d_model = x.shape
    depth = ffn1_w_gate.shape[0]
    d_conv = conv_w.shape[-1]
    rank, d_gla_k = w_alpha_up.shape[-2:]
    d_gla_v = w_gla_out.shape[-2]
    hk, hv = d_gla_k // GLA_HEADS, d_gla_v // GLA_HEADS
    q_off = 3 * d_conv
    k_off = q_off + d_gla_k
    v_off = k_off + d_gla_k
    r_off = v_off + d_gla_v
    a_off = r_off + d_gla_v
    g_off = a_off + rank
    m = batch * seq

    xs = x.reshape(m, d_model)
    for l in range(depth):
        xs, h = _ffn(xs, ffn1_norm_g[l][None], ffn1_w_gate[l].astype(BF16), ffn1_w_up[l], ffn1_w_down[l],
                     mix_norm_g[l][None], "x+norm")

        wt = jnp.transpose(w_in[l])
        proj, w2_gate, w2_up = _in_proj(
            h, wt, [(0, a_off), (g_off, 2 * d_model)],
            [(ffn2_w_gate[l], (d_model, IN_PROJ_SIDE_COLS)), (ffn2_w_up[l], (d_model, IN_PROJ_SIDE_COLS))])
        la = _loggate(h, wt, a_off, w_alpha_up[l], b_alpha[l][None])

        proj3 = proj.reshape(batch, seq, proj.shape[1])
        ua = _conv_branch(proj3, conv_w[l], conv_b[l][None], d_conv)
        gla_steps = batch * (seq // (CHUNK * GLA_CHUNKS_PER_STEP))
        ub, w2_down, w_mix, w_conv, w_gla = _gla(
            proj3, la.reshape(batch, seq, d_gla_k), gla_norm_g[l][None],
            [(ffn2_w_down[l], (GLA_SIDE_ROWS, d_model)),
             (w_mix_out[l], (d_model // gla_steps, d_model)),
             (w_conv_out[l], (d_conv // gla_steps, d_model)),
             (w_gla_out[l], (d_gla_v // gla_steps, d_model))],
            q_off=q_off, k_off=k_off, v_off=v_off, r_off=r_off, hk=hk, hv=hv)
        merged = _merge(ua.reshape(m, d_conv), ub.reshape(m, d_gla_v), w_conv, w_gla, proj, a_off, b_merge[l])
        xs = _matmul_res(merged, w_mix, xs)

        emit = "norm" if l == depth - 1 else "x"
        xs = _ffn(xs, ffn2_norm_g[l][None], w2_gate, w2_up, w2_down, final_norm_g[None], emit)
    return xs.reshape(batch, seq, d_model)
```

```python
import functools

import jax
import jax.numpy as jnp
import numpy as np
from jax import lax
from jax.experimental import pallas as pl
from jax.experimental.pallas import tpu as pltpu

F32 = jnp.float32
BF16 = jnp.bfloat16

EPS = 1e-6
FFN_RES = 0.5
CHUNK = 64
CONV_W = 3
GLA_HEADS = 4
GATE_TEMP = 16.0
N_LEVELS = 6
GLA_CHUNKS_PER_STEP = 2
FFN_DOWN_COLS = 512
NORM_ROWS = 16
NORM_UNROLL = 8
IN_PROJ_SIDE_COLS = 128
GLA_SIDE_ROWS = 256

V7X_VMEM_LIMIT_BYTES = 58 * 1024 * 1024
V7X_VMEM_LIMIT_LARGE_BYTES = 62 * 1024 * 1024
LANES = 128
SUBLANES = 8
V7X_MXU_COLS = 256


def _params(*sem, vmem_limit_bytes=V7X_VMEM_LIMIT_BYTES):
    return pltpu.CompilerParams(dimension_semantics=sem, vmem_limit_bytes=vmem_limit_bytes)


def _rms(x, g):
    ms = jnp.mean(x * x, axis=-1, keepdims=True)
    return x * lax.rsqrt(ms + EPS) * g


def _split_bf16(x):
    hi = x.astype(BF16)
    lo = (x - hi.astype(F32)).astype(BF16)
    return hi, lo


def _ffn_body(x_hbm, g_ref, wg_ref, wu_ref, wd_ref, gn_ref, *refs, emit, bm):
    if emit == "x+norm":
        o_hbm, hn_hbm, acc_ref, h_ref, rs_ref, sem_x, sem_o, sem_h = refs
    else:
        o_hbm, acc_ref, h_ref, rs_ref, sem_x, sem_o = refs
    i, f = pl.program_id(0), pl.program_id(1)
    n_i, n_f = pl.num_programs(0), pl.num_programs(1)
    slot = lax.rem(i, 2)
    acc = acc_ref.at[slot]
    d = h_ref.shape[1]
    n_slabs = bm // NORM_ROWS

    def tile(t):
        return pl.ds(pl.multiple_of(t * bm, bm), bm)

    def x_copy(t, s):
        return pltpu.make_async_copy(x_hbm.at[tile(t)], acc_ref.at[s], sem_x.at[s])

    def y_copy(t, s):
        return pltpu.make_async_copy(acc_ref.at[s], o_hbm.at[tile(t)], sem_o.at[s])

    def norm_copy(t, half):
        hb = bm // 2
        dst_rows = pl.ds(pl.multiple_of(t * bm + half * hb, hb), hb)
        return pltpu.make_async_copy(h_ref.at[pl.ds(half * hb, hb)], hn_hbm.at[dst_rows], sem_h.at[half])

    def rows(s):
        return pl.ds(pl.multiple_of(s * NORM_ROWS, NORM_ROWS), NORM_ROWS)

    @pl.when(f == 0)
    def _():
        @pl.when(i == 0)
        def _():
            x_copy(0, 0).start()

        x_copy(i, slot).wait()
        if emit == "x+norm":
            @pl.when(i > 0)
            def _():
                norm_copy(i - 1, 0).wait()
                norm_copy(i - 1, 1).wait()
                y_copy(i - 1, 1 - slot).start()

        g = g_ref[...]

        def slab(s, carry):
            h_ref[rows(s), :] = _rms(acc[rows(s), :], g).astype(BF16)
            return carry

        lax.fori_loop(0, n_slabs, slab, 0, unroll=NORM_UNROLL)

    @pl.when(f == 1)
    def _():
        @pl.when(i > 0)
        def _():
            y_copy(i - 1, 1 - slot).wait()

        @pl.when(i + 1 < n_i)
        def _():
            x_copy(i + 1, 1 - slot).start()

    h = h_ref[...]
    gate = jnp.dot(h, wg_ref[...].astype(BF16), preferred_element_type=F32)
    up = jnp.dot(h, wu_ref[...].astype(BF16), preferred_element_type=F32)
    act = (FFN_RES * (gate * jax.nn.sigmoid(gate) * up)).astype(BF16)
    for n0 in range(0, d, FFN_DOWN_COLS):
        sl = slice(n0, n0 + FFN_DOWN_COLS)
        acc[:, sl] += jnp.dot(act, wd_ref[:, sl].astype(BF16), preferred_element_type=F32)

    @pl.when(f == n_f - 1)
    def _():
        if emit != "x":
            gn = gn_ref[...]

            def stat_slab(s, carry):
                y = acc[rows(s), :]
                rs_ref[rows(s), :] = lax.rsqrt(jnp.mean(y * y, axis=-1, keepdims=True) + EPS)
                return carry

            lax.fori_loop(0, n_slabs, stat_slab, 0, unroll=NORM_UNROLL)

            def norm_slab(s, carry):
                normed = acc[rows(s), :] * rs_ref[rows(s), :] * gn
                if emit == "norm":
                    acc[rows(s), :] = normed
                else:
                    h_ref[rows(s), :] = normed.astype(BF16)
                return carry

            if emit == "norm":
                lax.fori_loop(0, n_slabs, norm_slab, 0, unroll=NORM_UNROLL)
            else:
                lax.fori_loop(0, n_slabs // 2, norm_slab, 0, unroll=NORM_UNROLL)
                norm_copy(i, 0).start()
                lax.fori_loop(n_slabs // 2, n_slabs, norm_slab, 0, unroll=NORM_UNROLL)
                norm_copy(i, 1).start()

        last_tile = i == n_i - 1
        if emit == "x+norm":
            @pl.when(last_tile)
            def _():
                y_copy(i, slot).start()
        else:
            y_copy(i, slot).start()

        @pl.when(last_tile)
        def _():
            y_copy(i, slot).wait()
            if emit == "x+norm":
                norm_copy(i, 0).wait()
                norm_copy(i, 1).wait()


def _ffn(x, g, wg, wu, wd, g_next, emit, *, bm=1024, bf=256):
    m, d = x.shape
    dff = wg.shape[1]
    assert m % bm == 0 and dff // bf >= 2
    vec = pl.BlockSpec((1, d), lambda i, f: (0, 0))
    hbm = pl.BlockSpec(memory_space=pl.ANY)
    out_specs, out_shape = hbm, jax.ShapeDtypeStruct((m, d), F32)
    scratch = [
        pltpu.VMEM((2, bm, d), F32),
        pltpu.VMEM((bm, d), BF16),
        pltpu.VMEM((bm, 1), F32),
        pltpu.SemaphoreType.DMA((2,)),
        pltpu.SemaphoreType.DMA((2,)),
    ]
    if emit == "x+norm":
        out_specs, out_shape = [hbm, hbm], [out_shape, jax.ShapeDtypeStruct((m, d), BF16)]
        scratch.append(pltpu.SemaphoreType.DMA((2,)))
    return pl.pallas_call(
        functools.partial(_ffn_body, emit=emit, bm=bm),
        grid=(m // bm, dff // bf),
        in_specs=[
            hbm,
            vec,
            pl.BlockSpec((d, bf), lambda i, f: (0, f)),
            pl.BlockSpec((d, bf), lambda i, f: (0, f)),
            pl.BlockSpec((bf, d), lambda i, f: (f, 0)),
            vec,
        ],
        out_specs=out_specs,
        out_shape=out_shape,
        scratch_shapes=scratch,
        compiler_params=_params("arbitrary", "arbitrary", vmem_limit_bytes=V7X_VMEM_LIMIT_LARGE_BYTES),
        name="ffn",
    )(x, g, wg, wu, wd, g_next)


class _SideCast:
    def __init__(self, src, block, grid, first_step=0):
        rows, cols = src.shape
        br, bc = block
        assert rows % br == 0 and cols % bc == 0 and (br == rows or bc == cols)
        n_blocks = (rows // br) * (cols // bc)
        assert grid[0] * grid[1] - first_step >= n_blocks
        n_inner = grid[1]

        def index(i, j):
            t = jnp.clip(i * n_inner + j - first_step, 0, n_blocks - 1)
            return (t, 0) if bc == cols else (0, t)

        self.src = src
        self.in_spec = pl.BlockSpec(block, index)
        self.out_spec = pl.BlockSpec(block, index)
        self.out_shape = jax.ShapeDtypeStruct(src.shape, BF16)

    @staticmethod
    def step(src_ref, dst_ref):
        dst_ref[...] = src_ref[...].astype(BF16)


def _mm_res_body(a_ref, w_ref, r_ref, o_ref):
    o_ref[...] = r_ref[...] + jnp.dot(a_ref[...], w_ref[...], preferred_element_type=F32)


def _matmul_res(a, w, res, *, bm=1024, bn=1024):
    m, k = a.shape
    n = w.shape[1]
    return pl.pallas_call(
        _mm_res_body,
        grid=(m // bm, n // bn),
        in_specs=[
            pl.BlockSpec((bm, k), lambda i, j: (i, 0)),
            pl.BlockSpec((k, bn), lambda i, j: (0, j)),
            pl.BlockSpec((bm, bn), lambda i, j: (i, j)),
        ],
        out_specs=pl.BlockSpec((bm, bn), lambda i, j: (i, j)),
        out_shape=jax.ShapeDtypeStruct((m, n), F32),
        compiler_params=_params("parallel", "arbitrary"),
        name="mix_out",
    )(a, w, res)


def _in_proj_body(a_ref, piece_ref, *refs, n_side):
    side_srcs, o_ref, side_dsts, wbf_ref = refs[:n_side], refs[n_side], refs[n_side + 1:-1], refs[-1]
    jj, i = pl.program_id(0), pl.program_id(1)
    for side_src, side_dst in zip(side_srcs, side_dsts):
        _SideCast.step(side_src, side_dst)

    piece = piece_ref.shape[0]

    def stage():
        wbf_ref[lax.rem(jj, 2), pl.ds(pl.multiple_of(i * piece, piece), piece), :] = piece_ref[...].astype(BF16)

    @pl.when(jj == 0)
    def _():
        stage()

    @pl.when(jj > 0)
    def _():
        stage()
        w = wbf_ref[lax.rem(jj + 1, 2)]
        nt = (((1,), (1,)), ((), ()))
        o_ref[...] = lax.dot_general(a_ref[...], w, nt, preferred_element_type=F32).astype(o_ref.dtype)


def _in_proj(a, wt, groups, sides, *, bm=1024, tile=1024):
    m, k = a.shape
    mi = m // bm
    piece = tile // mi
    assert piece % LANES == 0 and V7X_MXU_COLS % piece == 0
    (off0, rows0), (off1, rows1) = groups
    assert off0 % SUBLANES == 0 and off1 % SUBLANES == 0 and rows0 % tile == 0 and rows1 % tile == 0
    n0 = rows0 // tile
    n_tiles = n0 + rows1 // tile
    grid = (n_tiles + 1, mi)

    def piece_index(jj, i):
        t = jnp.minimum(jj, n_tiles - 1)
        p = jnp.where(jj < n_tiles, i, mi - 1)
        row = jnp.where(t < n0, off0 + t * tile, off1 + (t - n0) * tile) + p * piece
        return pl.multiple_of(row, SUBLANES), 0

    in_specs = [
        pl.BlockSpec((bm, k), lambda jj, i: (jnp.where(jj > 0, i, 0), 0)),
        pl.BlockSpec((pl.Element(piece), pl.Element(k)), piece_index),
    ]
    out_specs = [pl.BlockSpec((bm, tile), lambda jj, i: (jnp.where(jj > 0, i, 0), jnp.maximum(jj - 1, 0)))]
    out_shape = [jax.ShapeDtypeStruct((m, n_tiles * tile), BF16)]
    args = [a, wt]
    for src, block in sides:
        side = _SideCast(src, block, grid, first_step=mi)
        in_specs.append(side.in_spec)
        args.append(side.src)
        out_specs.append(side.out_spec)
        out_shape.append(side.out_shape)
    return pl.pallas_call(
        functools.partial(_in_proj_body, n_side=len(sides)),
        grid=grid,
        in_specs=in_specs,
        out_specs=out_specs,
        out_shape=out_shape,
        scratch_shapes=[pltpu.VMEM((2, tile, k), BF16)],
        compiler_params=_params("arbitrary", "arbitrary", vmem_limit_bytes=V7X_VMEM_LIMIT_LARGE_BYTES),
        name="in_proj",
    )(*args)


def _conv_body(cb_ref, cc_ref, cu_ref, w_ref, b_ref, o_ref):
    p = cc_ref[0].astype(F32) * cu_ref[0].astype(F32)
    row = lax.broadcasted_iota(jnp.int32, p.shape, 0)
    p1 = jnp.where(row >= 1, pltpu.roll(p, 1, axis=0), 0.0)
    p2 = jnp.where(row >= 2, pltpu.roll(p, 2, axis=0), 0.0)
    w = w_ref[...]
    conv = b_ref[...] + w[0:1, :] * p2
    conv = conv + w[1:2, :] * p1
    conv = conv + w[2:3, :] * p
    o_ref[0] = (cb_ref[0].astype(F32) * conv).astype(o_ref.dtype)


def _conv_branch(proj3, conv_w, conv_b, d_conv, *, bc=256):
    b, s, _ = proj3.shape
    nb = d_conv // bc
    return pl.pallas_call(
        _conv_body,
        grid=(b, nb),
        in_specs=[
            pl.BlockSpec((1, s, bc), lambda i, j: (i, 0, j)),
            pl.BlockSpec((1, s, bc), lambda i, j: (i, 0, nb + j)),
            pl.BlockSpec((1, s, bc), lambda i, j: (i, 0, 2 * nb + j)),
            pl.BlockSpec((CONV_W, bc), lambda i, j: (0, j)),
            pl.BlockSpec((1, bc), lambda i, j: (0, j)),
        ],
        out_specs=pl.BlockSpec((1, s, bc), lambda i, j: (i, 0, j)),
        out_shape=jax.ShapeDtypeStruct((b, s, d_conv), BF16),
        compiler_params=_params("parallel", "parallel"),
        name="conv",
    )(proj3, proj3, proj3, conv_w, conv_b)


def _loggate_body(h_ref, wl_ref, wup_ref, ba_ref, o_ref):
    nt = (((1,), (1,)), ((), ()))
    a = lax.dot_general(h_ref[...], wl_ref[...].astype(BF16), nt, preferred_element_type=F32)
    a_hi, a_lo = _split_bf16(a)
    w_hi, w_lo = _split_bf16(wup_ref[...])
    z = (jnp.dot(a_hi, w_hi, preferred_element_type=F32) + jnp.dot(a_hi, w_lo, preferred_element_type=F32)
         + jnp.dot(a_lo, w_hi, preferred_element_type=F32)) + ba_ref[...]
    o_ref[...] = (jnp.minimum(z, 0.0) - jnp.log(1.0 + jnp.exp(-jnp.abs(z)))) * (1.0 / GATE_TEMP)


def _loggate(h, wt, row_off, w_up, b_alpha, *, bm=1024):
    m, d = h.shape
    rank, dk = w_up.shape
    return pl.pallas_call(
        _loggate_body,
        grid=(m // bm,),
        in_specs=[
            pl.BlockSpec((bm, d), lambda i: (i, 0)),
            pl.BlockSpec((pl.Element(rank), pl.Element(d)), lambda i: (row_off, 0)),
            pl.BlockSpec((rank, dk), lambda i: (0, 0)),
            pl.BlockSpec((1, dk), lambda i: (0, 0)),
        ],
        out_specs=pl.BlockSpec((bm, dk), lambda i: (i, 0)),
        out_shape=jax.ShapeDtypeStruct((m, dk), F32),
        compiler_params=_params("parallel"),
        name="loggate",
    )(h, wt, w_up, b_alpha)


def _gla_tables():
    c = CHUNK
    i = np.arange(c)[:, None]
    t = np.arange(c)[None, :]
    blocks = [(t <= i), (t > i)]
    masks = []
    for lvl in range(N_LEVELS):
        p = c >> lvl
        half = p // 2
        mid = (i // p) * p + half
        upper = i >= mid
        blocks.append(np.where(upper, (t > mid) & (t <= i), (t > i) & (t <= mid)))
        j = t
        same_parent = (i // p) == (j // p)
        masks.append(same_parent & (((i % p) >= half) != ((j % p) >= half)))
    expo = np.concatenate(blocks, axis=0).astype(np.float32)
    mask = np.stack(masks).astype(np.float32)
    return expo, mask


def _gla_body(q_ref, k_ref, v_ref, r_ref, la_ref, gn_ref, expo_ref, mask_ref, *refs, scale, hk, hv, n_side):
    c = CHUNK
    side_srcs, o_ref, side_dsts, state_ref = refs[:n_side], refs[n_side], refs[n_side + 1:-1], refs[-1]
    for side_src, side_dst in zip(side_srcs, side_dsts):
        _SideCast.step(side_src, side_dst)
    nt = (((1,), (1,)), ((), ()))
    tn = (((0,), (0,)), ((), ()))

    @pl.when(pl.program_id(1) == 0)
    def _():
        state_ref[...] = jnp.zeros_like(state_ref)

    row = lax.broadcasted_iota(jnp.int32, (c, c), 0)
    col = lax.broadcasted_iota(jnp.int32, (c, c), 1)
    eye = row == col
    for h in range(GLA_HEADS):
        ks = slice(h * hk, (h + 1) * hk)
        vs = slice(h * hv, (h + 1) * hv)
        state = state_ref[h]
        for cc in range(GLA_CHUNKS_PER_STEP):
            rs = slice(cc * c, (cc + 1) * c)
            q = q_ref[0, rs, ks].astype(F32) * scale
            k = k_ref[0, rs, ks].astype(F32)
            v = v_ref[0, rs, vs]

            hi, lo = _split_bf16(la_ref[0, rs, ks])
            expo = jnp.dot(expo_ref[...], jnp.concatenate([hi, lo], axis=0), preferred_element_type=F32)
            e_all = jnp.exp(expo)
            e_l = e_all[0:c]
            e_rest = e_all[c:2 * c]

            scores = jnp.where(eye, jnp.sum(q * k, axis=-1, keepdims=True), 0.0)
            for lvl in range(N_LEVELS):
                e = e_all[(2 + lvl) * c:(3 + lvl) * c]
                p = lax.dot_general((q * e).astype(BF16), (k * e).astype(BF16), nt, preferred_element_type=F32)
                scores = scores + mask_ref[lvl] * p

            o = jnp.dot(scores.astype(BF16), v, preferred_element_type=F32)
            o = o + lax.dot_general((q * e_l).astype(BF16), state.astype(BF16), nt, preferred_element_type=F32)

            k_dec = (k * e_rest).astype(BF16)
            state = state * e_l[c - 1:c, :] + lax.dot_general(v, k_dec, tn, preferred_element_type=F32)

            r = r_ref[0, rs, vs].astype(F32)
            o_ref[0, rs, vs] = (_rms(o, gn_ref[...]) * (r * jax.nn.sigmoid(r))).astype(o_ref.dtype)
        state_ref[h] = state


def _gla(proj3, la3, gn, sides, *, q_off, k_off, v_off, r_off, hk, hv):
    b, s, _ = proj3.shape
    rows = CHUNK * GLA_CHUNKS_PER_STEP
    nc = s // rows
    dk, dv = GLA_HEADS * hk, GLA_HEADS * hv
    expo, mask = _gla_tables()
    expo2 = np.concatenate([expo, expo], axis=1)
    body = functools.partial(_gla_body, scale=float(hk) ** -0.5, hk=hk, hv=hv, n_side=len(sides))
    sides = [_SideCast(src, block, (b, nc)) for src, block in sides]
    return pl.pallas_call(
        body,
        grid=(b, nc),
        in_specs=[
            pl.BlockSpec((1, rows, dk), lambda i, c: (i, c, q_off // dk)),
            pl.BlockSpec((1, rows, dk), lambda i, c: (i, c, k_off // dk)),
            pl.BlockSpec((1, rows, dv), lambda i, c: (i, c, v_off // dv)),
            pl.BlockSpec((1, rows, dv), lambda i, c: (i, c, r_off // dv)),
            pl.BlockSpec((1, rows, dk), lambda i, c: (i, c, 0)),
            pl.BlockSpec((1, hv), lambda i, c: (0, 0)),
            pl.BlockSpec(expo2.shape, lambda i, c: (0, 0)),
            pl.BlockSpec(mask.shape, lambda i, c: (0, 0, 0)),
            *[side.in_spec for side in sides],
        ],
        out_specs=[pl.BlockSpec((1, rows, dv), lambda i, c: (i, c, 0)), *[side.out_spec for side in sides]],
        out_shape=[jax.ShapeDtypeStruct((b, s, dv), BF16), *[side.out_shape for side in sides]],
        scratch_shapes=[pltpu.VMEM((GLA_HEADS, hv, hk), F32)],
        compiler_params=_params("arbitrary", "arbitrary"),
        name="gla",
    )(proj3, proj3, proj3, proj3, la3, gn, jnp.asarray(expo2, dtype=BF16), jnp.asarray(mask),
      *[side.src for side in sides])


def _merge_body(ua_ref, ub_ref, wa_ref, wb_ref, ga_ref, gb_ref, bm_ref, o_ref):
    ya = jnp.dot(ua_ref[...], wa_ref[...], preferred_element_type=F32)
    yb = jnp.dot(ub_ref[...], wb_ref[...], preferred_element_type=F32)
    bias = bm_ref[...]
    sa = jax.nn.sigmoid(ga_ref[...].astype(F32) + bias[0:1, :])
    sb = jax.nn.sigmoid(gb_ref[...].astype(F32) + bias[1:2, :])
    o_ref[...] = (sa * ya + sb * yb).astype(o_ref.dtype)


def _merge(ua, ub, wa, wb, proj, gate_col, b_merge, *, bm=1024, bn=1024):
    m, kk = ua.shape
    n = wa.shape[1]
    nb = n // bn
    g0 = gate_col // bn
    assert gate_col % bn == 0
    return pl.pallas_call(
        _merge_body,
        grid=(m // bm, nb),
        in_specs=[
            pl.BlockSpec((bm, kk), lambda i, j: (i, 0)),
            pl.BlockSpec((bm, kk), lambda i, j: (i, 0)),
            pl.BlockSpec((kk, bn), lambda i, j: (0, j)),
            pl.BlockSpec((kk, bn), lambda i, j: (0, j)),
            pl.BlockSpec((bm, bn), lambda i, j: (i, g0 + j)),
            pl.BlockSpec((bm, bn), lambda i, j: (i, g0 + nb + j)),
            pl.BlockSpec((2, bn), lambda i, j: (0, j)),
        ],
        out_specs=pl.BlockSpec((bm, bn), lambda i, j: (i, j)),
        out_shape=jax.ShapeDtypeStruct((m, n), BF16),
        compiler_params=_params("parallel", "arbitrary"),
        name="merge",
    )(ua, ub, wa, wb, proj, proj, b_merge)


def kernel(x, ffn1_norm_g, ffn1_w_gate, ffn1_w_up, ffn1_w_down, mix_norm_g, w_in, conv_w, conv_b, w_conv_out,
           w_alpha_up, b_alpha, gla_norm_g, w_gla_out, b_merge, w_mix_out, ffn2_norm_g, ffn2_w_gate, ffn2_w_up,
           ffn2_w_down, final_norm_g):
    batch, seq, d_model = x.shape
    depth = ffn1_w_gate.shape[0]
    d_conv = conv_w.shape[-1]
    rank, d_gla_k = w_alpha_up.shape[-2:]
    d_gla_v = w_gla_out.shape[-2]
    hk, hv = d_gla_k // GLA_HEADS, d_gla_v // GLA_HEADS
    q_off = 3 * d_conv
    k_off = q_off + d_gla_k
    v_off = k_off + d_gla_k
    r_off = v_off + d_gla_v
    a_off = r_off + d_gla_v
    g_off = a_off + rank
    m = batch * seq

    xs = x.reshape(m, d_model)
    for l in range(depth):
        xs, h = _ffn(xs, ffn1_norm_g[l][None], ffn1_w_gate[l].astype(BF16), ffn1_w_up[l], ffn1_w_down[l],
                     mix_norm_g[l][None], "x+norm")

        wt = jnp.transpose(w_in[l])
        proj, w2_gate, w2_up = _in_proj(
            h, wt, [(0, a_off), (g_off, 2 * d_model)],
            [(ffn2_w_gate[l], (d_model, IN_PROJ_SIDE_COLS)), (ffn2_w_up[l], (d_model, IN_PROJ_SIDE_COLS))])
        la = _loggate(h, wt, a_off, w_alpha_up[l], b_alpha[l][None])

        proj3 = proj.reshape(batch, seq, proj.shape[1])
        ua = _conv_branch(proj3, conv_w[l], conv_b[l][None], d_conv)
        gla_steps = batch * (seq // (CHUNK * GLA_CHUNKS_PER_STEP))
        ub, w2_down, w_mix, w_conv, w_gla = _gla(
            proj3, la.reshape(batch, seq, d_gla_k), gla_norm_g[l][None],
            [(ffn2_w_down[l], (GLA_SIDE_ROWS, d_model)),
             (w_mix_out[l], (d_model // gla_steps, d_model)),
             (w_conv_out[l], (d_conv // gla_steps, d_model)),
             (w_gla_out[l], (d_gla_v // gla_steps, d_model))],
            q_off=q_off, k_off=k_off, v_off=v_off, r_off=r_off, hk=hk, hv=hv)
        merged = _merge(ua.reshape(m, d_conv), ub.reshape(m, d_gla_v), w_conv, w_gla, proj, a_off, b_merge[l])
        xs = _matmul_res(merged, w_mix, xs)

        emit = "norm" if l == depth - 1 else "x"
        xs = _ffn(xs, ffn2_norm_g[l][None], w2_gate, w2_up, w2_down, final_norm_g[None], emit)
    return xs.reshape(batch, seq, d_model)
```
